```python
import jax
import jax.numpy as jnp
from jax import lax
import numpy as np

D_MODEL = 1024
BATCH = 1
SEQ = 16384
DEPTH = 2
DEC_BATCH = 32
DEC_SEQ = 4
PAST_LEN = 16384
PAGE_SIZE = 128

HEAD_DIM = 64
MIX_WIDTH = D_MODEL
H_FOX = MIX_WIDTH // (4 * HEAD_DIM)
D_CONV = MIX_WIDTH // 4
H_NSA = MIX_WIDTH // (2 * HEAD_DIM)
N_KV_NSA = 2
CONV_K = 3
CMP_STRIDE = 16
CMP_BLOCK = 2 * CMP_STRIDE
SLC_BLOCK = 64
TOP_N = 16
N_LOCAL_BLOCKS = 2
SLC_COVER_W = (1.0, 2.0, 2.0, 2.0, 1.0)
WINDOW = 512
Q_BLOCK = 128
D_FF = 2816
N_EXPERTS = 8
TOP_K = 2
N_IN = 3 * H_FOX * HEAD_DIM + H_FOX + 3 * D_CONV + H_NSA * HEAD_DIM + 6 * N_KV_NSA * HEAD_DIM + 3 * H_NSA
N_KV_ROWS = 2 * H_FOX + 4 * N_KV_NSA
N_WIN_ROWS = 2 * N_KV_NSA
ALPHA = (2.0 * DEPTH) ** 0.25
BETA = (8.0 * DEPTH) ** -0.25
LN_EPS = 1e-5
FORCE_BONUS = 1e4

kernel_name = 'hybrid_fox_conv_nsa_decoder_step'


def _layernorm(x, g, b):
    xf = x.astype(jnp.float32)
    mu = jnp.mean(xf, axis=-1, keepdims=True)
    var = jnp.mean(jnp.square(xf - mu), axis=-1, keepdims=True)
    return ((xf - mu) * lax.rsqrt(var + LN_EPS) * g + b).astype(x.dtype)


def _masked_softmax(s, valid):
    s = jnp.where(valid, s, -jnp.inf)
    m = jnp.max(s, axis=-1, keepdims=True)
    m = jnp.where(jnp.isfinite(m), m, 0.0)
    e = jnp.exp(s - m)
    return e / jnp.maximum(jnp.sum(e, axis=-1, keepdims=True), 1e-30)


def _alibi_slopes(n):
    return jnp.exp2(-8.0 * jnp.arange(1, n + 1, dtype=jnp.float32) / n)


def _split_proj(proj):
    sizes = ((H_FOX * HEAD_DIM,) * 3 + (H_FOX,) + (D_CONV,) * 3
             + (H_NSA * HEAD_DIM,) + (N_KV_NSA * HEAD_DIM,) * 6 + (3 * H_NSA,))
    out, off = [], 0
    for sz in sizes:
        out.append(proj[..., off:off + sz])
        off += sz
    return out


def _fox_attention(q, k_all, v_all, c_all, P):
    N, T, H, Dh = q.shape
    L = k_all.shape[1]
    qb_len = min(Q_BLOCK, T)
    nb = T // qb_len
    scale = Dh ** -0.5
    c_k = jnp.swapaxes(c_all, 1, 2)
    q_blocks = jnp.moveaxis(q.reshape(N, nb, qb_len, H, Dh), 1, 0)
    cq_blocks = jnp.moveaxis(c_k[:, :, P:].reshape(N, H, nb, qb_len), 2, 0)
    key_pos = jnp.arange(L)

    def block(args):
        i, qb, cq = args
        qpos = P + i * qb_len + jnp.arange(qb_len)
        s = jnp.einsum('nqhd,nkhd->nhqk', qb, k_all).astype(jnp.float32) * scale
        s = s + (cq[..., None] - c_k[:, :, None, :])
        p = _masked_softmax(s, key_pos[None, :] <= qpos[:, None])
        return jnp.einsum('nhqk,nkhd->nqhd', p.astype(v_all.dtype), v_all)

    o = lax.map(block, (jnp.arange(nb), q_blocks, cq_blocks))
    return jnp.moveaxis(o, 0, 1).reshape(N, T, H * Dh)


def _compress(rows, w):
    N, L, G, Dh = rows.shape
    n_half = L // CMP_STRIDE
    halves = rows[:, :n_half * CMP_STRIDE].reshape(N, n_half, CMP_STRIDE, G, Dh)
    halves = jnp.swapaxes(halves, 2, 3).reshape(N, n_half, G, CMP_STRIDE * Dh)
    hl = CMP_STRIDE * Dh
    return (jnp.einsum('nigx,xd->nigd', halves[:, :-1], w[:hl])
            + jnp.einsum('nigx,xd->nigd', halves[:, 1:], w[hl:]))


def _to_blocks(rows):
    N, L, G, Dh = rows.shape
    nbs = -(-L // SLC_BLOCK)
    rows = jnp.pad(rows, ((0, 0), (0, nbs * SLC_BLOCK - L), (0, 0), (0, 0)))
    return jnp.transpose(rows.reshape(N, nbs, SLC_BLOCK, G, Dh), (0, 3, 1, 2, 4))


def _nsa_attention(q, gates, kc_all, vc_all, ks_all, vs_all, kw_ext, vw_ext, w_cmp_k, w_cmp_v, P):
    N, T, H, Dh = q.shape
    G = N_KV_NSA
    R = H // G
    L = kc_all.shape[1]
    scale = Dh ** -0.5
    slopes = _alibi_slopes(H).reshape(G, R)[None, :, :, None, None]
    ck = _compress(kc_all, w_cmp_k)
    cv = _compress(vc_all, w_cmp_v)
    nbc = ck.shape[1]
    cmp_end = jnp.arange(nbc) * CMP_STRIDE + (CMP_BLOCK - 1)
    ks_b = _to_blocks(ks_all)
    vs_b = _to_blocks(vs_all)
    nbs = ks_b.shape[2]
    n_sel = min(TOP_N, nbs)
    ratio = SLC_BLOCK // CMP_STRIDE
    pad_after = ratio * nbs + ratio - nbc - 1
    blk = jnp.arange(nbs)
    n_idx = jnp.arange(N)[:, None, None, None]
    g_idx = jnp.arange(G)[None, :, None, None]
    qb_len = min(Q_BLOCK, T)
    nb = T // qb_len
    q_blocks = jnp.moveaxis(q.reshape(N, nb, qb_len, G, R, Dh), 1, 0)
    g_blocks = jnp.moveaxis(gates.reshape(N, nb, qb_len, G, R, 3), 1, 0)

    def block(args):
        i, qg, gb = args
        qpos = P + i * qb_len + jnp.arange(qb_len)
        s = jnp.einsum('nqgrd,nigd->ngrqi', qg, ck).astype(jnp.float32) * scale
        s = s - slopes * (qpos[:, None] - cmp_end[None, :])
        p_c = _masked_softmax(s, cmp_end[None, :] <= qpos[:, None])
        o_c = jnp.einsum('ngrqi,nigd->nqgrd', p_c.astype(cv.dtype), cv)
        imp = jnp.pad(jnp.sum(p_c, axis=2), ((0, 0), (0, 0), (0, 0), (1, pad_after)))
        score = SLC_COVER_W[0] * imp[..., 0:ratio * nbs:ratio]
        for m in range(1, len(SLC_COVER_W)):
            score = score + SLC_COVER_W[m] * imp[..., m:m + ratio * nbs:ratio]
        cur = qpos // SLC_BLOCK
        valid_s = blk[None, :] <= cur[:, None]
        forced = (blk[None, :] == 0) | (blk[None, :] >= cur[:, None] - (N_LOCAL_BLOCKS - 1))
        score = jnp.where(valid_s, score + jnp.where(forced, FORCE_BONUS, 0.0), -FORCE_BONUS)
        _, idx = lax.top_k(score, n_sel)
        k_sel = ks_b[n_idx, g_idx, idx]
        v_sel = vs_b[n_idx, g_idx, idx]
        kpos = idx[..., None] * SLC_BLOCK + jnp.arange(SLC_BLOCK)
        dist = qpos[:, None, None] - kpos[:, :, None]
        s = jnp.einsum('nqgrd,ngqkpd->ngrqkp', qg, k_sel).astype(jnp.float32) * scale
        s = s - slopes[..., None] * dist
        p_s = _masked_softmax(s.reshape(N, G, R, qb_len, n_sel * SLC_BLOCK),
                              (dist >= 0).reshape(N, G, 1, qb_len, n_sel * SLC_BLOCK))
        p_s = p_s.reshape(N, G, R, qb_len, n_sel, SLC_BLOCK)
        o_s = jnp.einsum('ngrqkp,ngqkpd->nqgrd', p_s.astype(v_sel.dtype), v_sel)
        start = i * qb_len
        kw = lax.dynamic_slice_in_dim(kw_ext, start, WINDOW + qb_len, axis=1)
        vw = lax.dynamic_slice_in_dim(vw_ext, start, WINDOW + qb_len, axis=1)
        wpos = P - WINDOW + start + jnp.arange(WINDOW + qb_len)
        d = qpos[:, None] - wpos[None, :]
        valid_w = (d >= 0) & (d <= WINDOW) & (wpos[None, :] >= 0)
        s = jnp.einsum('nqgrd,nkgd->ngrqk', qg, kw).astype(jnp.float32) * scale - slopes * d
        p_w = _masked_softmax(s, valid_w)
        o_w = jnp.einsum('ngrqk,nkgd->nqgrd', p_w.astype(vw.dtype), vw)
        gsig = jax.nn.sigmoid(gb.astype(jnp.float32)).astype(q.dtype)
        o = gsig[..., 0:1] * o_c + gsig[..., 1:2] * o_s + gsig[..., 2:3] * o_w
        return o.reshape(N, qb_len, G * R * Dh)

    o = lax.map(block, (jnp.arange(nb), q_blocks, g_blocks))
    return jnp.moveaxis(o, 0, 1).reshape(N, T, H * Dh)


def _mixer(h, fk_p, fv_p, logf_p, conv_p, kc_p, vc_p, ks_p, vs_p, win_p,
           w_in, b_f, conv_w, w_cmp_k, w_cmp_v, w_out):
    N, T, _ = h.shape
    P = fk_p.shape[1]
    (q_f, k_f, v_f, f_lin, b_g, c_g, u_in, q_n, kc, vc, ks, vs, kw, vw, g_n) = _split_proj(h @ w_in)
    hd = lambda a: a.reshape(N, T, -1, HEAD_DIM)
    q_f, k_f, v_f = hd(q_f), hd(k_f), hd(v_f)
    q_n, kc, vc, ks, vs, kw, vw = hd(q_n), hd(kc), hd(vc), hd(ks), hd(vs), hd(kw), hd(vw)
    logf = jax.nn.log_sigmoid((f_lin + b_f).astype(jnp.float32))
    c_all = jnp.cumsum(jnp.concatenate([logf_p.astype(jnp.float32), logf], axis=1), axis=1)
    o_fox = _fox_attention(q_f, jnp.concatenate([fk_p, k_f], axis=1),
                           jnp.concatenate([fv_p, v_f], axis=1), c_all, P)
    u = c_g * u_in
    u_all = jnp.concatenate([conv_p.astype(u.dtype), u], axis=1)
    y = conv_w[0] * u_all[:, 0:T]
    for j in range(1, CONV_K):
        y = y + conv_w[j] * u_all[:, j:j + T]
    o_conv = b_g * y
    Wp = win_p.shape[1]
    win_all = jnp.concatenate([win_p, jnp.concatenate([kw, vw], axis=2)], axis=1)
    ext = jnp.concatenate([jnp.zeros((N, WINDOW - Wp, N_WIN_ROWS, HEAD_DIM), win_all.dtype), win_all], axis=1)
    o_nsa = _nsa_attention(q_n, g_n.reshape(N, T, H_NSA, 3),
                           jnp.concatenate([kc_p, kc], axis=1), jnp.concatenate([vc_p, vc], axis=1),
                           jnp.concatenate([ks_p, ks], axis=1), jnp.concatenate([vs_p, vs], axis=1),
                           ext[:, :, :N_KV_NSA], ext[:, :, N_KV_NSA:], w_cmp_k, w_cmp_v, P)
    out = jnp.concatenate([o_fox, o_conv, o_nsa], axis=-1) @ w_out
    new_kv = jnp.concatenate([k_f, v_f, kc, vc, ks, vs], axis=2)
    keep = min(WINDOW, Wp + T)
    return out, new_kv, logf.astype(h.dtype), u_all[:, T:], win_all[:, Wp + T - keep:]


def _swiglu(x, w1, w3, w2):
    return (jax.nn.silu(x @ w1) * (x @ w3)) @ w2


def _moe(x, router_w, w1, w3, w2):
    probs = jax.nn.softmax((x @ router_w).astype(jnp.float32), axis=-1)
    top_p, top_i = lax.top_k(probs, TOP_K)
    top_p = top_p / jnp.sum(top_p, axis=-1, keepdims=True)
    gate = jnp.einsum('ntk,ntke->nte', top_p, jax.nn.one_hot(top_i, N_EXPERTS, dtype=jnp.float32)).astype(x.dtype)
    out = jnp.zeros_like(x)
    for e in range(N_EXPERTS):
        out = out + gate[..., e:e + 1] * _swiglu(x, w1[e], w3[e], w2[e])
    return out


def _trunk(x, get_past, w_in, b_f, conv_w, w_cmp_k, w_cmp_v, w_out, ln1_g, ln1_b, ln2_g, ln2_b,
           dense_w1, dense_w3, dense_w2, router_w, moe_w1, moe_w3, moe_w2):
    kv_rows, logf_rows, conv_rows, win_rows = [], [], [], []
    for l in range(DEPTH):
        mix, nkv, nlogf, nconv, nwin = _mixer(x, *get_past(l), w_in[l], b_f[l], conv_w[l],
                                               w_cmp_k[l], w_cmp_v[l], w_out[l])
        x = _layernorm(ALPHA * x + mix, ln1_g[l], ln1_b[l])
        if l % 2 == 0:
            ffn = _swiglu(x, dense_w1[l // 2], dense_w3[l // 2], dense_w2[l // 2])
        else:
            ffn = _moe(x, router_w[l // 2], moe_w1[l // 2], moe_w3[l // 2], moe_w2[l // 2])
        x = _layernorm(ALPHA * x + ffn, ln2_g[l], ln2_b[l])
        kv_rows.append(nkv)
        logf_rows.append(nlogf)
        conv_rows.append(nconv)
        win_rows.append(nwin)
    return x, jnp.stack(kv_rows), jnp.stack(logf_rows), jnp.stack(conv_rows), jnp.stack(win_rows)


def _paged_rows(pool, l, page_table, lo, hi):
    rows = pool[l, page_table, :, lo:hi]
    return rows.reshape(rows.shape[0], -1, hi - lo, rows.shape[-1])


def setup_inputs(seed: int = 0) -> dict:
    key = jax.random.key(seed)
    k = jax.random.split(key, 24)
    f32 = jnp.float32
    n_pages = PAST_LEN // PAGE_SIZE
    n_used = DEC_BATCH * n_pages
    n_pool = n_used + max(1, n_used // 4)
    w_buf = min(WINDOW, PAST_LEN)
    n_dense = (DEPTH + 1) // 2
    n_moe = DEPTH // 2

    def nrm(kk, shape, scale=1.0):
        return jax.random.normal(kk, shape, f32) * scale

    page_table = jax.random.permutation(k[0], n_pool)[:n_used].reshape(DEC_BATCH, n_pages).astype(jnp.int32)
    return {
        'x_prompt': nrm(k[1], (BATCH, SEQ, D_MODEL)),
        'x_sample': nrm(k[2], (DEC_BATCH, DEC_SEQ, D_MODEL)),
        'cache_kv': nrm(k[3], (DEPTH, n_pool, PAGE_SIZE, N_KV_ROWS, HEAD_DIM)),
        'cache_logf': jax.nn.log_sigmoid(2.0 + nrm(k[4], (DEPTH, n_pool, PAGE_SIZE, H_FOX))),
        'state_conv': nrm(k[5], (DEPTH, DEC_BATCH, CONV_K - 1, D_CONV)),
        'state_win': nrm(k[6], (DEPTH, DEC_BATCH, w_buf, N_WIN_ROWS, HEAD_DIM)),
        'page_table': page_table,
        'w_in': nrm(k[7], (DEPTH, D_MODEL, N_IN), D_MODEL ** -0.5),
        'b_f': 2.0 + nrm(k[8], (DEPTH, H_FOX), 0.1),
        'conv_w': nrm(k[9], (DEPTH, CONV_K, D_CONV), CONV_K ** -0.5),
        'w_cmp_k': nrm(k[10], (DEPTH, CMP_BLOCK * HEAD_DIM, HEAD_DIM), (CMP_BLOCK * HEAD_DIM) ** -0.5),
        'w_cmp_v': nrm(k[11], (DEPTH, CMP_BLOCK * HEAD_DIM, HEAD_DIM), (CMP_BLOCK * HEAD_DIM) ** -0.5),
        'w_out': nrm(k[12], (DEPTH, MIX_WIDTH, D_MODEL), BETA * MIX_WIDTH ** -0.5),
        'ln1_g': 1.0 + nrm(k[13], (DEPTH, D_MODEL), 0.05),
        'ln1_b': nrm(k[14], (DEPTH, D_MODEL), 0.02),
        'ln2_g': 1.0 + nrm(k[15], (DEPTH, D_MODEL), 0.05),
        'ln2_b': nrm(k[16], (DEPTH, D_MODEL), 0.02),
        'dense_w1': nrm(k[17], (n_dense, D_MODEL, D_FF), D_MODEL ** -0.5),
        'dense_w3': nrm(k[18], (n_dense, D_MODEL, D_FF), D_MODEL ** -0.5),
        'dense_w2': nrm(k[19], (n_dense, D_FF, D_MODEL), BETA * D_FF ** -0.5),
        'router_w': nrm(k[20], (n_moe, D_MODEL, N_EXPERTS), D_MODEL ** -0.5),
        'moe_w1': nrm(k[21], (n_moe, N_EXPERTS, D_MODEL, D_FF), D_MODEL ** -0.5),
        'moe_w3': nrm(k[22], (n_moe, N_EXPERTS, D_MODEL, D_FF), D_MODEL ** -0.5),
        'moe_w2': nrm(k[23], (n_moe, N_EXPERTS, D_FF, D_MODEL), BETA * D_FF ** -0.5),
    }


def reference(x_prompt, x_sample, cache_kv, cache_logf, state_conv, state_win, page_table,
              w_in, b_f, conv_w, w_cmp_k, w_cmp_v, w_out, ln1_g, ln1_b, ln2_g, ln2_b,
              dense_w1, dense_w3, dense_w2, router_w, moe_w1, moe_w3, moe_w2):
    nbp = x_prompt.shape[0]
    dt = x_prompt.dtype
    G = N_KV_NSA

    def prompt_past(l):
        z = lambda r: jnp.zeros((nbp, 0, r, HEAD_DIM), dt)
        return (z(H_FOX), z(H_FOX), jnp.zeros((nbp, 0, H_FOX), dt),
                jnp.zeros((nbp, CONV_K - 1, D_CONV), dt), z(G), z(G), z(G), z(G), z(N_WIN_ROWS))

    def sample_past(l):
        r0 = 2 * H_FOX
        logf = cache_logf[l, page_table]
        return (_paged_rows(cache_kv, l, page_table, 0, H_FOX),
                _paged_rows(cache_kv, l, page_table, H_FOX, r0),
                logf.reshape(logf.shape[0], -1, H_FOX),
                state_conv[l],
                _paged_rows(cache_kv, l, page_table, r0, r0 + G),
                _paged_rows(cache_kv, l, page_table, r0 + G, r0 + 2 * G),
                _paged_rows(cache_kv, l, page_table, r0 + 2 * G, r0 + 3 * G),
                _paged_rows(cache_kv, l, page_table, r0 + 3 * G, r0 + 4 * G),
                state_win[l])

    y_prompt, kv_prompt, logf_prompt, conv_prompt, win_prompt = _trunk(
        x_prompt, prompt_past, w_in, b_f, conv_w, w_cmp_k, w_cmp_v, w_out, ln1_g, ln1_b, ln2_g, ln2_b,
        dense_w1, dense_w3, dense_w2, router_w, moe_w1, moe_w3, moe_w2)
    y_sample, kv_sample, logf_sample, conv_sample, win_sample = _trunk(
        x_sample, sample_past, w_in, b_f, conv_w, w_cmp_k, w_cmp_v, w_out, ln1_g, ln1_b, ln2_g, ln2_b,
        dense_w1, dense_w3, dense_w2, router_w, moe_w1, moe_w3, moe_w2)
    return (y_prompt, y_sample, kv_prompt, logf_prompt, conv_prompt, win_prompt,
            kv_sample, logf_sample, conv_sample, win_sample)
```

```python
import functools

import numpy as np
import jax
import jax.numpy as jnp
from jax import lax
from jax.experimental import pallas as pl
from jax.experimental.pallas import tpu as pltpu

f32 = jnp.float32
bf16 = jnp.bfloat16

HEAD_DIM = 64
H_FOX = 4
D_CONV = 256
H_NSA = 8
N_KV_NSA = 2
HEADS_PER_GROUP = H_NSA // N_KV_NSA
CONV_K = 3
CMP_STRIDE = 16
SLC_BLOCK = 64
TOP_N = 16
N_LOCAL_BLOCKS = 2
SLC_COVER_W = (1.0, 2.0, 2.0, 2.0, 1.0)
WINDOW = 512
N_EXPERTS = 8
TOP_K = 2
LN_EPS = 1e-5
FORCE_BONUS = 1e4
NEG = -1e30
LANES = 128
SUBLANES = 8
VMEM_LIMIT = 56 * 1024 * 1024
HIGHEST = lax.Precision.HIGHEST

_C_QF, _C_KF, _C_VF, _C_FL = 0, 256, 512, 768
_C_BG, _C_CG, _C_UIN, _C_QN = 772, 1028, 1284, 1540
_C_KC, _C_VC, _C_KS, _C_VS, _C_KW, _C_VW, _C_GN, _C_END = 2052, 2180, 2308, 2436, 2564, 2692, 2820, 2844
R_QN, R_QF, R_BG, R_CG, R_UIN, R_SM, R_END = 0, 512, 768, 1024, 1280, 1536, 1664
SM_GN = 4
T_KF, T_VF, T_KC, T_VC, T_KS, T_VS, T_KW, T_VW, T_FL, T_END = 0, 256, 512, 640, 768, 896, 1024, 1152, 1280, 1288
N_KV_FEAT = 1024


def _cparams(sem):
    return pltpu.CompilerParams(dimension_semantics=sem, vmem_limit_bytes=VMEM_LIMIT)


def _nt_dot(a, b):
    return lax.dot_general(a, b, (((1,), (1,)), ((), ())), preferred_element_type=f32)


def _log_sigmoid(x):
    return -(jnp.maximum(-x, 0.0) + jnp.log1p(jnp.exp(-jnp.abs(x))))


def _sigmoid(x):
    return 1.0 / (1.0 + jnp.exp(-x))


def _proj_kernel(x_ref, wr_ref, wt_ref, pr_ref, pt_ref):
    xb = x_ref[...].astype(bf16)
    pr_ref[...] = jnp.dot(xb, wr_ref[...], preferred_element_type=f32)
    pt_ref[...] = _nt_dot(wt_ref[...], xb)


def _project(x, wr, wt, tm):
    m, d = x.shape
    cr, ct = wr.shape[1], wt.shape[0]
    return pl.pallas_call(
        _proj_kernel,
        grid=(m // tm,),
        in_specs=[pl.BlockSpec((tm, d), lambda i: (i, 0)),
                  pl.BlockSpec((d, cr), lambda i: (0, 0)),
                  pl.BlockSpec((ct, d), lambda i: (0, 0))],
        out_specs=[pl.BlockSpec((tm, cr), lambda i: (i, 0)),
                   pl.BlockSpec((ct, tm), lambda i: (0, i))],
        out_shape=[jax.ShapeDtypeStruct((m, cr), f32), jax.ShapeDtypeStruct((ct, m), f32)],
        compiler_params=_cparams(("parallel",)),
        name="project",
    )(x, wr, wt)


def _logf_cumsum_kernel(f_ref, bf_ref, tri_ref, logf_ref, c_ref, carry_ref):
    @pl.when(pl.program_id(0) == 0)
    def _():
        carry_ref[...] = jnp.zeros_like(carry_ref)

    carry = carry_ref[...]
    n_sub = f_ref.shape[1] // LANES
    for s in range(n_sub):
        sl = slice(s * LANES, (s + 1) * LANES)
        lf = _log_sigmoid(f_ref[:, sl] + bf_ref[...])
        logf_ref[:, sl] = lf
        cs = jnp.dot(lf, tri_ref[...], precision=HIGHEST, preferred_element_type=f32) + carry
        c_ref[:, sl] = cs
        carry = jnp.broadcast_to(cs[:, LANES - 1:LANES], carry.shape)
    carry_ref[...] = carry


def _logf_cumsum(pt, bf8, tc, segment=LANES):
    t = pt.shape[1]
    ii, jj = np.meshgrid(np.arange(LANES), np.arange(LANES), indexing="ij")
    tri = jnp.asarray(((ii <= jj) & (ii // segment == jj // segment)).astype(np.float32))
    return pl.pallas_call(
        _logf_cumsum_kernel,
        grid=(t // tc,),
        in_specs=[pl.BlockSpec((SUBLANES, tc), lambda j: (T_FL // SUBLANES, j)),
                  pl.BlockSpec((SUBLANES, 1), lambda j: (0, 0)),
                  pl.BlockSpec((LANES, LANES), lambda j: (0, 0))],
        out_specs=[pl.BlockSpec((SUBLANES, tc), lambda j: (0, j)),
                   pl.BlockSpec((SUBLANES, tc), lambda j: (0, j))],
        out_shape=[jax.ShapeDtypeStruct((SUBLANES, t), f32)] * 2,
        scratch_shapes=[pltpu.VMEM((SUBLANES, LANES), f32)],
        compiler_params=_cparams(("arbitrary",)),
        name="logf_cumsum",
    )(pt, bf8, tri)


def _alibi_slope(h):
    return float(2.0 ** (-8.0 * (h + 1) / H_NSA))


def _flash_kernel(*refs, mode, n_heads, tq, tk):
    if mode == "fox":
        q_ref, kt_ref, vt_ref, c_ref, cq_ref, o_ref, m_ref, l_ref, acc_ref = refs
    elif mode == "slc":
        q_ref, kt_ref, vt_ref, sel_ref, o_ref, m_ref, l_ref, acc_ref = refs
    else:
        q_ref, kt_ref, vt_ref, o_ref, m_ref, l_ref, acc_ref = refs
    i, j = pl.program_id(0), pl.program_id(1)
    nk = pl.num_programs(1)

    @pl.when(j == 0)
    def _():
        m_ref[...] = jnp.full_like(m_ref, NEG)
        l_ref[...] = jnp.zeros_like(l_ref)
        acc_ref[...] = jnp.zeros_like(acc_ref)

    if mode == "win":
        kv_tile = jnp.maximum(i - 1 + j, 0)
        active = jnp.logical_or(i > 0, j > 0)
    else:
        kv_tile = j
        active = j <= i

    @pl.when(active)
    def _():
        q0 = i * tq
        qpos = q0 + lax.broadcasted_iota(jnp.int32, (tq, 1), 0)
        kpos = kv_tile * tk + lax.broadcasted_iota(jnp.int32, (1, tk), 1)
        if mode == "win":
            d = qpos - kpos
            neg = jnp.where(jnp.logical_or(d < 0, d > WINDOW), NEG, 0.0)
        else:
            neg = jnp.where(kpos > qpos, NEG, 0.0)
        rel = (kpos - q0).astype(f32)
        if mode == "slc":
            nbs = sel_ref.shape[1] // N_KV_NSA
            blk_of_key = lax.shift_right_logical(kpos, 6)
            expand = (lax.broadcasted_iota(jnp.int32, (nbs, 1), 0) == blk_of_key).astype(bf16)
        for h in range(n_heads):
            kv = h if mode == "fox" else h // HEADS_PER_GROUP
            qh = (q_ref[:, h * HEAD_DIM:(h + 1) * HEAD_DIM] * 0.125).astype(bf16)
            kth = kt_ref[kv * HEAD_DIM:(kv + 1) * HEAD_DIM, :].astype(bf16)
            vth = vt_ref[kv * HEAD_DIM:(kv + 1) * HEAD_DIM, :].astype(bf16)
            s = jnp.dot(qh, kth, preferred_element_type=f32)
            if mode == "fox":
                bias = cq_ref[h:h + 1, 0:1] - c_ref[h:h + 1, :]
                mask = neg
            else:
                bias = _alibi_slope(h) * rel
                if mode == "slc":
                    if h % HEADS_PER_GROUP == 0:
                        selg = sel_ref[:, kv * nbs:(kv + 1) * nbs].astype(bf16)
                        hit = jnp.dot(selg, expand, preferred_element_type=f32)
                        mask_g = jnp.where(hit > 0.5, neg, NEG)
                    mask = mask_g
                else:
                    mask = neg
            s = s + bias + mask
            m_old = m_ref[h]
            m_new = jnp.maximum(m_old, jnp.max(s, axis=1, keepdims=True))
            alpha = jnp.exp(m_old - m_new)
            p = jnp.exp(s - m_new)
            l_ref[h] = alpha * l_ref[h] + jnp.sum(p, axis=1, keepdims=True)
            acc_ref[h] = alpha * acc_ref[h] + _nt_dot(p.astype(bf16), vth)
            m_ref[h] = m_new

    @pl.when(j == nk - 1)
    def _():
        for h in range(n_heads):
            o_ref[:, h * HEAD_DIM:(h + 1) * HEAD_DIM] = acc_ref[h] / l_ref[h]


def _flash(mode, pr, pt, q_col, kt_row, vt_row, tq, extra=()):
    t = pr.shape[0]
    n_heads = H_FOX if mode == "fox" else H_NSA
    n_kv = H_FOX if mode == "fox" else N_KV_NSA
    qw, kw = n_heads * HEAD_DIM, n_kv * HEAD_DIM
    tk = tq
    nq = t // tq
    nk = 2 if mode == "win" else nq
    if mode == "win":
        kv_map = lambda i, j: jnp.maximum(i - 1 + j, 0)
    else:
        kv_map = lambda i, j: jnp.minimum(j, i)
    in_specs = [pl.BlockSpec((tq, qw), lambda i, j: (i, q_col // qw)),
                pl.BlockSpec((kw, tk), lambda i, j: (kt_row // kw, kv_map(i, j))),
                pl.BlockSpec((kw, tk), lambda i, j: (vt_row // kw, kv_map(i, j)))]
    if mode == "fox":
        (ct,) = extra
        in_specs += [pl.BlockSpec((SUBLANES, tk), lambda i, j: (0, kv_map(i, j))),
                     pl.BlockSpec((SUBLANES, LANES), lambda i, j: (0, i * (tq // LANES)))]
        args = (pr, pt, pt, ct, ct)
    elif mode == "slc":
        (sel,) = extra
        in_specs += [pl.BlockSpec((tq, sel.shape[1]), lambda i, j: (i, 0))]
        args = (pr, pt, pt, sel)
    else:
        args = (pr, pt, pt)
    return pl.pallas_call(
        functools.partial(_flash_kernel, mode=mode, n_heads=n_heads, tq=tq, tk=tk),
        grid=(nq, nk),
        in_specs=in_specs,
        out_specs=pl.BlockSpec((tq, qw), lambda i, j: (i, 0)),
        out_shape=jax.ShapeDtypeStruct((t, qw), f32),
        scratch_shapes=[pltpu.VMEM((n_heads, tq, 1), f32), pltpu.VMEM((n_heads, tq, 1), f32),
                        pltpu.VMEM((n_heads, tq, HEAD_DIM), f32)],
        compiler_params=_cparams(("parallel", "arbitrary")),
        name="flash_" + mode,
    )(*args)


def _compress_tile(a, at_ref, wc):
    tb = a.shape[1]
    at_ref[...] = a.T
    n_half = tb // CMP_STRIDE
    acc = jnp.zeros((n_half, 2 * HEAD_DIM), f32)
    for r in range(CMP_STRIDE):
        rows = at_ref[pl.ds(r, n_half, stride=CMP_STRIDE), :]
        acc = acc + jnp.dot(rows.astype(bf16), wc(r), preferred_element_type=f32)
    return acc


def _compress_kernel(a_ref, wc_ref, o_ref, at_ref):
    o_ref[0] = _compress_tile(a_ref[...], at_ref, lambda r: wc_ref[0, r])


def _compress(pt, wc, tb):
    t = pt.shape[1]
    base = T_KC // HEAD_DIM
    return pl.pallas_call(
        _compress_kernel,
        grid=(2 * N_KV_NSA, t // tb),
        in_specs=[pl.BlockSpec((HEAD_DIM, tb), lambda k, j: (base + k, j)),
                  pl.BlockSpec((1, CMP_STRIDE, HEAD_DIM, 2 * HEAD_DIM), lambda k, j: (k, 0, 0, 0))],
        out_specs=pl.BlockSpec((1, tb // CMP_STRIDE, 2 * HEAD_DIM), lambda k, j: (k, j, 0)),
        out_shape=jax.ShapeDtypeStruct((2 * N_KV_NSA, t // CMP_STRIDE, 2 * HEAD_DIM), f32),
        scratch_shapes=[pltpu.VMEM((tb, HEAD_DIM), f32)],
        compiler_params=_cparams(("parallel", "parallel")),
        name="compress",
    )(pt, wc)


def _cmp_select_kernel(q_ref, hh_ref, cov_ref, oc_ref, sel_ref, idx_ref, *, past, tq, nbs):
    n_half = hh_ref.shape[2]
    nbsp = cov_ref.shape[1]
    t0 = pl.program_id(1) * tq
    qpos = past + t0 + lax.broadcasted_iota(jnp.int32, (tq, 1), 0)
    cmp_end = lax.broadcasted_iota(jnp.int32, (1, n_half), 1) * CMP_STRIDE + (2 * CMP_STRIDE - 1)
    valid_c = cmp_end <= qpos
    dist = (qpos - cmp_end).astype(f32)
    blk = lax.broadcasted_iota(jnp.int32, (1, nbsp), 1)
    blk_f = blk.astype(f32)
    cur = lax.shift_right_logical(qpos, 6)
    valid_s = blk <= cur
    forced = jnp.logical_or(blk == 0, blk >= cur - (N_LOCAL_BLOCKS - 1))
    lane = lax.broadcasted_iota(jnp.int32, (1, LANES), 1)
    idx_acc = jnp.zeros((tq, LANES), f32)
    for g in range(N_KV_NSA):
        hk = hh_ref[0, g]
        hv = hh_ref[0, N_KV_NSA + g]
        ck = hk[:, :HEAD_DIM] + pltpu.roll(hk, n_half - 1, 0)[:, HEAD_DIM:]
        cv = hv[:, :HEAD_DIM] + pltpu.roll(hv, n_half - 1, 0)[:, HEAD_DIM:]
        ckb, cvb = ck.astype(bf16), cv.astype(bf16)
        imp = jnp.zeros((tq, n_half), f32)
        for r in range(HEADS_PER_GROUP):
            h = g * HEADS_PER_GROUP + r
            qh = (q_ref[:, h * HEAD_DIM:(h + 1) * HEAD_DIM] * 0.125).astype(bf16)
            s = _nt_dot(qh, ckb) - _alibi_slope(h) * dist
            s = jnp.where(valid_c, s, -jnp.inf)
            m = jnp.max(s, axis=1, keepdims=True)
            m = jnp.where(m == -jnp.inf, 0.0, m)
            e = jnp.exp(s - m)
            p = e / jnp.maximum(jnp.sum(e, axis=1, keepdims=True), 1e-30)
            oc_ref[:, h * HEAD_DIM:(h + 1) * HEAD_DIM] = jnp.dot(p.astype(bf16), cvb, preferred_element_type=f32)
            imp = imp + p
        score = jnp.dot(imp, cov_ref[...], precision=HIGHEST, preferred_element_type=f32)
        score = jnp.where(valid_s, score + jnp.where(forced, FORCE_BONUS, 0.0), -FORCE_BONUS)
        score = jnp.where(blk < nbs, score, -jnp.inf)
        selm = jnp.zeros((tq, nbsp), f32)
        for k in range(min(TOP_N, nbs)):
            mx = jnp.max(score, axis=1, keepdims=True)
            ix = jnp.min(jnp.where(score == mx, blk_f, 1e9), axis=1, keepdims=True)
            hit = blk_f == ix
            selm = jnp.where(hit, 1.0, selm)
            score = jnp.where(hit, -jnp.inf, score)
            idx_acc = jnp.where(lane == g * TOP_N + k, ix, idx_acc)
        sel_ref[:, g * nbsp:(g + 1) * nbsp] = selm
    idx_ref[...] = idx_acc.astype(jnp.int32)


def _cover_matrix(n_half, nbs, nbsp):
    cov = np.zeros((n_half, nbsp), np.float32)
    ratio = SLC_BLOCK // CMP_STRIDE
    for jb in range(nbs):
        for mm, w in enumerate(SLC_COVER_W):
            c = ratio * jb + mm - 1
            if 0 <= c < n_half - 1:
                cov[c, jb] += w
    return jnp.asarray(cov)


def _cmp_select(q, q_col, hh, past, t_len, tq):
    hh = hh.reshape(-1, 2 * N_KV_NSA, hh.shape[-2], hh.shape[-1])
    n = hh.shape[0]
    n_half = hh.shape[2]
    tp = q.shape[0] // n
    nbs = -(-(past + t_len) // SLC_BLOCK)
    nbsp = -(-nbs // LANES) * LANES
    cov = _cover_matrix(n_half, nbs, nbsp)
    qw = H_NSA * HEAD_DIM
    hh4 = hh
    nt = tp // tq
    return pl.pallas_call(
        functools.partial(_cmp_select_kernel, past=past, tq=tq, nbs=nbs),
        grid=(n, nt),
        in_specs=[pl.BlockSpec((tq, qw), lambda b, i: (b * nt + i, q_col // qw)),
                  pl.BlockSpec((1, 2 * N_KV_NSA, n_half, 2 * HEAD_DIM), lambda b, i: (b, 0, 0, 0)),
                  pl.BlockSpec((n_half, nbsp), lambda b, i: (0, 0))],
        out_specs=[pl.BlockSpec((tq, qw), lambda b, i: (b * nt + i, 0)),
                   pl.BlockSpec((tq, N_KV_NSA * nbsp), lambda b, i: (b * nt + i, 0)),
                   pl.BlockSpec((tq, LANES), lambda b, i: (b * nt + i, 0))],
        out_shape=[jax.ShapeDtypeStruct((n * tp, qw), f32),
                   jax.ShapeDtypeStruct((n * tp, N_KV_NSA * nbsp), f32),
                   jax.ShapeDtypeStruct((n * tp, LANES), jnp.int32)],
        compiler_params=_cparams(("parallel", "parallel")),
        name="cmp_select",
    )(q, hh4, cov)


def _conv_kernel(b_ref, c_ref, u_ref, ch_ref, uh_ref, st_ref, w_ref, o_ref, last_ref):
    i = pl.program_id(0)
    u = c_ref[...] * u_ref[...]
    halo = jnp.where(i == 0, st_ref[...], ch_ref[...] * uh_ref[...])
    ue = jnp.concatenate([halo, u], axis=0)
    u1 = pltpu.roll(ue, 1, 0)[SUBLANES:]
    u2 = pltpu.roll(ue, 2, 0)[SUBLANES:]
    w = w_ref[...]
    y = w[0:1] * u2 + w[1:2] * u1 + w[2:3] * u
    o_ref[...] = b_ref[...] * y
    last_ref[...] = u[u.shape[0] - SUBLANES:]


def _conv_prompt(pr, state8, w8, tm):
    t = pr.shape[0]
    cb = lambda off: off // D_CONV
    hb = tm // SUBLANES
    return pl.pallas_call(
        _conv_kernel,
        grid=(t // tm,),
        in_specs=[pl.BlockSpec((tm, D_CONV), lambda i: (i, cb(R_BG))),
                  pl.BlockSpec((tm, D_CONV), lambda i: (i, cb(R_CG))),
                  pl.BlockSpec((tm, D_CONV), lambda i: (i, cb(R_UIN))),
                  pl.BlockSpec((SUBLANES, D_CONV), lambda i: (jnp.maximum(i * hb - 1, 0), cb(R_CG))),
                  pl.BlockSpec((SUBLANES, D_CONV), lambda i: (jnp.maximum(i * hb - 1, 0), cb(R_UIN))),
                  pl.BlockSpec((SUBLANES, D_CONV), lambda i: (0, 0)),
                  pl.BlockSpec((SUBLANES, D_CONV), lambda i: (0, 0))],
        out_specs=[pl.BlockSpec((tm, D_CONV), lambda i: (i, 0)),
                   pl.BlockSpec((SUBLANES, D_CONV), lambda i: (0, 0))],
        out_shape=[jax.ShapeDtypeStruct((t, D_CONV), f32), jax.ShapeDtypeStruct((SUBLANES, D_CONV), f32)],
        compiler_params=_cparams(("arbitrary",)),
        name="conv",
    )(pr, pr, pr, pr, pr, state8, w8)


def _layernorm(z, g, b):
    mu = jnp.mean(z, axis=-1, keepdims=True)
    zc = z - mu
    var = jnp.mean(zc * zc, axis=-1, keepdims=True)
    return zc * lax.rsqrt(var + LN_EPS) * g + b


def _outproj_kernel(x_ref, fox_ref, conv_ref, oc_ref, os_ref, ow_ref, sm_ref, ex_ref, w_ref, g_ref, b_ref,
                    o_ref, *, alpha):
    sig = _sigmoid(sm_ref[...])
    nsa = None
    for br, br_ref in enumerate((oc_ref, os_ref, ow_ref)):
        gate = jnp.dot(sig, ex_ref[br], precision=HIGHEST, preferred_element_type=f32)
        term = gate * br_ref[...]
        nsa = term if nsa is None else nsa + term
    a0, a1 = H_FOX * HEAD_DIM, H_FOX * HEAD_DIM + D_CONV
    mix = jnp.dot(fox_ref[...].astype(bf16), w_ref[0:a0, :], preferred_element_type=f32)
    mix = mix + jnp.dot(conv_ref[...].astype(bf16), w_ref[a0:a1, :], preferred_element_type=f32)
    mix = mix + jnp.dot(nsa.astype(bf16), w_ref[a1:, :], preferred_element_type=f32)
    o_ref[...] = _layernorm(alpha * x_ref[...] + mix, g_ref[...], b_ref[...])


def _gate_expand():
    ex = np.zeros((3, LANES, H_NSA * HEAD_DIM), np.float32)
    for h in range(H_NSA):
        for br in range(3):
            ex[br, SM_GN + h * 3 + br, h * HEAD_DIM:(h + 1) * HEAD_DIM] = 1.0
    return jnp.asarray(ex)


def _outproj_ln(x, o_fox, o_conv, o_c, o_s, o_w, small, sm_col, w_out, g, b, alpha, tm):
    m, d = x.shape
    row = lambda w: pl.BlockSpec((tm, w), lambda i: (i, 0))
    full = lambda shp: pl.BlockSpec(shp, lambda i: (0,) * len(shp))
    nw = H_NSA * HEAD_DIM
    return pl.pallas_call(
        functools.partial(_outproj_kernel, alpha=alpha),
        grid=(m // tm,),
        in_specs=[row(d), row(H_FOX * HEAD_DIM), row(D_CONV), row(nw), row(nw), row(nw),
                  pl.BlockSpec((tm, LANES), lambda i: (i, sm_col // LANES)),
                  full((3, LANES, nw)), full(w_out.shape), full((1, d)), full((1, d))],
        out_specs=row(d),
        out_shape=jax.ShapeDtypeStruct((m, d), f32),
        compiler_params=_cparams(("parallel",)),
        name="outproj_ln",
    )(x, o_fox, o_conv, o_c, o_s, o_w, small, _gate_expand(), w_out, g.reshape(1, d), b.reshape(1, d))


def _swiglu_partial(xb, w1, w3, w2):
    h1 = jnp.dot(xb, w1, preferred_element_type=f32)
    h3 = jnp.dot(xb, w3, preferred_element_type=f32)
    hh = (h1 * _sigmoid(h1) * h3).astype(bf16)
    return jnp.dot(hh, w2, preferred_element_type=f32)


def _ffn_kernel(x_ref, w1_ref, w3_ref, w2_ref, g_ref, b_ref, o_ref, acc_ref, *, alpha):
    f = pl.program_id(1)

    @pl.when(f == 0)
    def _():
        acc_ref[...] = jnp.zeros_like(acc_ref)

    acc_ref[...] += _swiglu_partial(x_ref[...].astype(bf16), w1_ref[...], w3_ref[...], w2_ref[...])

    @pl.when(f == pl.num_programs(1) - 1)
    def _():
        o_ref[...] = _layernorm(alpha * x_ref[...] + acc_ref[...], g_ref[...], b_ref[...])


def _ffn_ln(x, w1, w3, w2, g, b, alpha, tm, tf):
    m, d = x.shape
    ff = w1.shape[1]
    return pl.pallas_call(
        functools.partial(_ffn_kernel, alpha=alpha),
        grid=(m // tm, ff // tf),
        in_specs=[pl.BlockSpec((tm, d), lambda i, f: (i, 0)),
                  pl.BlockSpec((d, tf), lambda i, f: (0, f)),
                  pl.BlockSpec((d, tf), lambda i, f: (0, f)),
                  pl.BlockSpec((tf, d), lambda i, f: (f, 0)),
                  pl.BlockSpec((1, d), lambda i, f: (0, 0)),
                  pl.BlockSpec((1, d), lambda i, f: (0, 0))],
        out_specs=pl.BlockSpec((tm, d), lambda i, f: (i, 0)),
        out_shape=jax.ShapeDtypeStruct((m, d), f32),
        scratch_shapes=[pltpu.VMEM((tm, d), f32)],
        compiler_params=_cparams(("parallel", "arbitrary")),
        name="ffn_ln",
    )(x, w1, w3, w2, g.reshape(1, d), b.reshape(1, d))


def _moe_kernel(x_ref, r_ref, w1_ref, w3_ref, w2_ref, g_ref, b_ref, o_ref, acc_ref, gate_ref, *, alpha):
    e, f = pl.program_id(1), pl.program_id(2)
    lane = lax.broadcasted_iota(jnp.int32, (1, LANES), 1)

    @pl.when(jnp.logical_and(e == 0, f == 0))
    def _():
        acc_ref[...] = jnp.zeros_like(acc_ref)
        logits = jnp.dot(x_ref[...], r_ref[...], precision=HIGHEST, preferred_element_type=f32)
        logits = jnp.where(lane < N_EXPERTS, logits, -jnp.inf)
        mx = jnp.max(logits, axis=1, keepdims=True)
        ex = jnp.exp(logits - mx)
        probs = ex / jnp.sum(ex, axis=1, keepdims=True)
        lane_f = lane.astype(f32)
        work = jnp.where(lane < N_EXPERTS, probs, -1.0)
        picked = jnp.zeros_like(probs)
        for _ in range(TOP_K):
            top = jnp.max(work, axis=1, keepdims=True)
            ix = jnp.min(jnp.where(work == top, lane_f, 1e9), axis=1, keepdims=True)
            hit = lane_f == ix
            picked = jnp.where(hit, probs, picked)
            work = jnp.where(hit, -1.0, work)
        gate_ref[...] = picked / jnp.sum(picked, axis=1, keepdims=True)

    part = _swiglu_partial(x_ref[...].astype(bf16), w1_ref[0], w3_ref[0], w2_ref[0])
    gate_e = jnp.sum(jnp.where(lane == e, gate_ref[...], 0.0), axis=1, keepdims=True)
    acc_ref[...] += gate_e * part

    @pl.when(jnp.logical_and(e == pl.num_programs(1) - 1, f == pl.num_programs(2) - 1))
    def _():
        o_ref[...] = _layernorm(alpha * x_ref[...] + acc_ref[...], g_ref[...], b_ref[...])


def _moe_ln(x, router, w1, w3, w2, g, b, alpha, tm, tf):
    m, d = x.shape
    ne, _, ff = w1.shape
    return pl.pallas_call(
        functools.partial(_moe_kernel, alpha=alpha),
        grid=(m // tm, ne, ff // tf),
        in_specs=[pl.BlockSpec((tm, d), lambda i, e, f: (i, 0)),
                  pl.BlockSpec((d, LANES), lambda i, e, f: (0, 0)),
                  pl.BlockSpec((1, d, tf), lambda i, e, f: (e, 0, f)),
                  pl.BlockSpec((1, d, tf), lambda i, e, f: (e, 0, f)),
                  pl.BlockSpec((1, tf, d), lambda i, e, f: (e, f, 0)),
                  pl.BlockSpec((1, d), lambda i, e, f: (0, 0)),
                  pl.BlockSpec((1, d), lambda i, e, f: (0, 0))],
        out_specs=pl.BlockSpec((tm, d), lambda i, e, f: (i, 0)),
        out_shape=jax.ShapeDtypeStruct((m, d), f32),
        scratch_shapes=[pltpu.VMEM((tm, d), f32), pltpu.VMEM((tm, LANES), f32)],
        compiler_params=_cparams(("parallel", "arbitrary", "arbitrary")),
        name="moe_ln",
    )(x, router, w1, w3, w2, g.reshape(1, d), b.reshape(1, d))


def _pack_layer_weights(w_in, b_f, conv_w, w_cmp_k, w_cmp_v):
    d = w_in.shape[0]
    cols = lambda a, b: w_in[:, a:b]
    small = jnp.concatenate([cols(_C_FL, _C_BG), cols(_C_GN, _C_END),
                             jnp.zeros((d, LANES - H_FOX - 3 * H_NSA), w_in.dtype)], axis=1)
    wr = jnp.concatenate([cols(_C_QN, _C_KC), cols(_C_QF, _C_KF), cols(_C_BG, _C_QN), small], axis=1)
    kv_cols = jnp.concatenate([cols(_C_KF, _C_FL), cols(_C_KC, _C_KW)], axis=1)
    win_cols = cols(_C_KW, _C_GN)
    fl = jnp.concatenate([cols(_C_FL, _C_BG), jnp.zeros((d, SUBLANES - H_FOX), w_in.dtype)], axis=1)
    wt = jnp.concatenate([kv_cols, win_cols, fl], axis=1).T
    wr_all = jnp.concatenate([wr, kv_cols, win_cols], axis=1)
    bf8 = jnp.concatenate([b_f, jnp.zeros((SUBLANES - H_FOX,), b_f.dtype)]).reshape(SUBLANES, 1)
    w8 = jnp.concatenate([conv_w, jnp.zeros((SUBLANES - CONV_K, D_CONV), conv_w.dtype)], axis=0)

    def wc_of(w):
        lo = w[:CMP_STRIDE * HEAD_DIM].reshape(CMP_STRIDE, HEAD_DIM, HEAD_DIM)
        hi = w[CMP_STRIDE * HEAD_DIM:].reshape(CMP_STRIDE, HEAD_DIM, HEAD_DIM)
        return jnp.concatenate([lo, hi], axis=-1)

    wck, wcv = wc_of(w_cmp_k), wc_of(w_cmp_v)
    wc = jnp.stack([wck] * N_KV_NSA + [wcv] * N_KV_NSA).astype(bf16)
    return dict(wr=wr.astype(bf16), wr_all=wr_all.astype(bf16), wt=wt.astype(bf16), bf8=bf8, w8=w8, wc=wc)


def _tile(t, want):
    return min(t, want)


def _prompt_mixer(x, pk, w_out_b, tiles):
    t = x.shape[0]
    pr, pt = _project(x, pk["wr"], pk["wt"], _tile(t, tiles["proj"]))
    logf_t, c_t = _logf_cumsum(pt, pk["bf8"], _tile(t, 2048))
    tq = _tile(t, tiles["flash"])
    o_fox = _flash("fox", pr, pt, R_QF, T_KF, T_VF, tq, (c_t,))
    hh = _compress(pt, pk["wc"], _tile(t, 2048))
    o_c, sel, _ = _cmp_select(pr, R_QN, hh, 0, t, _tile(t, tiles["cmp"]))
    o_s = _flash("slc", pr, pt, R_QN, T_KS, T_VS, tq, (sel,))
    o_w = _flash("win", pr, pt, R_QN, T_KW, T_VW, _tile(t, WINDOW))
    state8 = jnp.zeros((SUBLANES, D_CONV), f32)
    o_conv, u_last = _conv_prompt(pr, state8, pk["w8"], _tile(t, tiles["proj"]))
    return pr, pt, logf_t, o_fox, o_conv, o_c, o_s, o_w, u_last


def _trunk_prompt(x, layers, alpha, tiles):
    t = x.shape[0]
    kv, logf, conv, win = [], [], [], []
    for lw in layers:
        pr, pt, logf_t, o_fox, o_conv, o_c, o_s, o_w, u_last = _prompt_mixer(x, lw["pk"], lw["w_out"], tiles)
        tm = _tile(t, tiles["row"])
        x = _outproj_ln(x, o_fox, o_conv, o_c, o_s, o_w, pr, R_SM, lw["w_out"], lw["ln1_g"], lw["ln1_b"], alpha, tm)
        if lw["moe"] is None:
            w1, w3, w2 = lw["dense"]
            x = _ffn_ln(x, w1, w3, w2, lw["ln2_g"], lw["ln2_b"], alpha, _tile(t, tiles["ffn"]), tiles["tf"])
        else:
            rw, w1, w3, w2 = lw["moe"]
            x = _moe_ln(x, rw, w1, w3, w2, lw["ln2_g"], lw["ln2_b"], alpha, _tile(t, tiles["ffn"]), tiles["tf"])
        kv.append(pt[:N_KV_FEAT].reshape(N_KV_FEAT // HEAD_DIM, HEAD_DIM, t))
        logf.append(logf_t[:H_FOX])
        conv.append(u_last[SUBLANES - (CONV_K - 1):])
        keep = min(WINDOW, t)
        win.append(pt[T_KW:T_FL, t - keep:].reshape(2 * N_KV_NSA, HEAD_DIM, keep))
    kv = jnp.transpose(jnp.stack(kv), (0, 3, 1, 2))[:, None]
    logf = jnp.transpose(jnp.stack(logf), (0, 2, 1))[:, None]
    conv = jnp.stack(conv)[:, None]
    win = jnp.transpose(jnp.stack(win), (0, 3, 1, 2))[:, None]
    return x, kv, logf, conv, win


TILES = dict(proj=512, flash=512, cmp=256, row=512, ffn=1024, tf=1408, pg=16)


def _prepare_layers(w_in, b_f, conv_w, w_cmp_k, w_cmp_v, w_out, ln1_g, ln1_b, ln2_g, ln2_b,
                    dense_w1, dense_w3, dense_w2, router_w, moe_w1, moe_w3, moe_w2):
    layers = []
    d = w_in.shape[1]
    for l in range(w_in.shape[0]):
        lw = dict(pk=_pack_layer_weights(w_in[l], b_f[l], conv_w[l], w_cmp_k[l], w_cmp_v[l]),
                  w_out=w_out[l].astype(bf16), ln1_g=ln1_g[l], ln1_b=ln1_b[l], ln2_g=ln2_g[l], ln2_b=ln2_b[l],
                  dense=None, moe=None)
        if l % 2 == 0:
            lw["dense"] = (dense_w1[l // 2].astype(bf16), dense_w3[l // 2].astype(bf16), dense_w2[l // 2].astype(bf16))
        else:
            rw = jnp.concatenate([router_w[l // 2], jnp.zeros((d, LANES - N_EXPERTS), f32)], axis=1)
            lw["moe"] = (rw, moe_w1[l // 2].astype(bf16), moe_w3[l // 2].astype(bf16), moe_w2[l // 2].astype(bf16))
        layers.append(lw)
    return layers


R_KV, R_WIN, R_ALL_END = R_END, R_END + N_KV_FEAT, R_END + N_KV_FEAT + 2 * N_KV_NSA * HEAD_DIM
PAGE = 128
ROW_FOX, ROW_CMP, ROW_SLC = 0, 2 * H_FOX, 2 * H_FOX + 2 * N_KV_NSA
FOX_W = H_FOX * HEAD_DIM
Q_ROWS = 16


def _online_update(s, v_t, m_ref, l_ref, acc_ref):
    m_old = m_ref[...]
    m_new = jnp.maximum(m_old, jnp.max(s, axis=1, keepdims=True))
    alpha = jnp.exp(m_old - m_new)
    p = jnp.exp(s - m_new)
    l_ref[...] = alpha * l_ref[...] + jnp.sum(p, axis=1, keepdims=True)
    acc_ref[...] = alpha * acc_ref[...] + _nt_dot(p.astype(bf16), v_t)
    m_ref[...] = m_new


def _decode_kernel(pt_ref, qbd_ref, knew_ref, vnew_ref, cq_ref, ck_ref, wc_ref, tri_ref, *rest, pg):
    kv_refs, cmp_refs, lf_refs = rest[:pg], rest[pg:2 * pg], rest[2 * pg:3 * pg]
    o_ref, hh_ref, m_ref, l_ref, acc_ref, carry_ref, at_ref = rest[3 * pg:]
    j = pl.program_id(1)

    @pl.when(j == 0)
    def _():
        m_ref[...] = jnp.full_like(m_ref, NEG)
        l_ref[...] = jnp.zeros_like(l_ref)
        acc_ref[...] = jnp.zeros_like(acc_ref)
        carry_ref[...] = jnp.zeros_like(carry_ref)

    qbd = (qbd_ref[0] * 0.125).astype(bf16)
    row_head = lax.rem(lax.broadcasted_iota(jnp.int32, (Q_ROWS, 1), 0), H_FOX)
    cq = cq_ref[0]
    for k in reversed(range(pg)):
        kt = kv_refs[k][0, 0, 0:H_FOX].reshape(FOX_W, PAGE).astype(bf16)
        vt = kv_refs[k][0, 0, H_FOX:2 * H_FOX].reshape(FOX_W, PAGE).astype(bf16)
        lf8 = jnp.concatenate([lf_refs[k][0, 0], jnp.zeros((SUBLANES - H_FOX, PAGE), f32)], axis=0)
        cs = jnp.dot(lf8, tri_ref[...], precision=HIGHEST, preferred_element_type=f32)
        tot = cs[:, PAGE - 1:PAGE]
        carry = carry_ref[...]
        suffix = carry + (tot - cs)
        carry_ref[...] = carry + tot
        bias = jnp.zeros((Q_ROWS, PAGE), f32)
        for h in range(H_FOX):
            bias = jnp.where(row_head == h, suffix[h:h + 1, :], bias)
        s = jnp.dot(qbd, kt, preferred_element_type=f32) + bias + cq
        _online_update(s, vt, m_ref, l_ref, acc_ref)
    n_half = pg * (PAGE // CMP_STRIDE)
    for kind in range(2 * N_KV_NSA):
        for k in range(pg):
            at_ref[kind, k * PAGE:(k + 1) * PAGE, :] = cmp_refs[k][0, 0, kind].T
        acc = jnp.zeros((n_half, 2 * HEAD_DIM), f32)
        for r in range(CMP_STRIDE):
            rows = at_ref[kind, pl.ds(r, n_half, stride=CMP_STRIDE), :]
            acc = acc + jnp.dot(rows.astype(bf16), wc_ref[kind, r], preferred_element_type=f32)
        hh_ref[0, kind] = acc

    @pl.when(j == pl.num_programs(1) - 1)
    def _():
        knew = knew_ref[0].astype(bf16)
        s = _nt_dot(qbd, knew)
        trow = lax.shift_right_logical(lax.broadcasted_iota(jnp.int32, (Q_ROWS, 1), 0), 2)
        tcol = lax.broadcasted_iota(jnp.int32, (1, SUBLANES), 1)
        s = jnp.where(tcol <= trow, s + (cq - ck_ref[0]), NEG)
        _online_update(s, vnew_ref[0].T.astype(bf16), m_ref, l_ref, acc_ref)
        o_ref[0] = acc_ref[...] / l_ref[...]


def _decode(l, page_table, qbd, knew, vnew, cq, ck, wc, cache_t, logf_t, pg):
    n, n_pages = page_table.shape
    steps = n_pages // pg
    tri = jnp.asarray(np.triu(np.ones((PAGE, PAGE), np.float32)))

    def page_of(b, j, k, pt):
        return pt[b, (steps - 1 - j) * pg + k]

    per_seq = lambda shp: pl.BlockSpec((1,) + shp, lambda b, j, pt: (b,) + (0,) * len(shp))
    const = lambda shp: pl.BlockSpec(shp, lambda b, j, pt: (0,) * len(shp))
    kv_specs = [pl.BlockSpec((1, 1, 2 * H_FOX, HEAD_DIM, PAGE),
                             lambda b, j, pt, k=k: (l, page_of(b, j, k, pt), ROW_FOX // (2 * H_FOX), 0, 0)) for k in range(pg)]
    cmp_specs = [pl.BlockSpec((1, 1, 2 * N_KV_NSA, HEAD_DIM, PAGE),
                              lambda b, j, pt, k=k: (l, page_of(b, j, k, pt), ROW_CMP // (2 * N_KV_NSA), 0, 0)) for k in range(pg)]
    lf_specs = [pl.BlockSpec((1, 1, H_FOX, PAGE), lambda b, j, pt, k=k: (l, page_of(b, j, k, pt), 0, 0)) for k in range(pg)]
    n_half = n_pages * (PAGE // CMP_STRIDE)
    grid_spec = pltpu.PrefetchScalarGridSpec(
        num_scalar_prefetch=1,
        grid=(n, steps),
        in_specs=[per_seq((Q_ROWS, FOX_W)), per_seq((SUBLANES, FOX_W)), per_seq((SUBLANES, FOX_W)),
                  per_seq((Q_ROWS, 1)), per_seq((Q_ROWS, SUBLANES)),
                  const(wc.shape), const((PAGE, PAGE))] + kv_specs + cmp_specs + lf_specs,
        out_specs=[per_seq((Q_ROWS, FOX_W)),
                   pl.BlockSpec((1, 2 * N_KV_NSA, pg * (PAGE // CMP_STRIDE), 2 * HEAD_DIM),
                                lambda b, j, pt: (b, 0, steps - 1 - j, 0))],
        scratch_shapes=[pltpu.VMEM((Q_ROWS, 1), f32), pltpu.VMEM((Q_ROWS, 1), f32), pltpu.VMEM((Q_ROWS, FOX_W), f32),
                        pltpu.VMEM((SUBLANES, PAGE), f32), pltpu.VMEM((2 * N_KV_NSA, pg * PAGE, HEAD_DIM), f32)])
    return pl.pallas_call(
        functools.partial(_decode_kernel, pg=pg),
        grid_spec=grid_spec,
        out_shape=[jax.ShapeDtypeStruct((n, Q_ROWS, FOX_W), f32),
                   jax.ShapeDtypeStruct((n, 2 * N_KV_NSA, n_half, 2 * HEAD_DIM), f32)],
        compiler_params=_cparams(("parallel", "arbitrary")),
        name="decode_fox_compress",
    )(page_table, qbd, knew, vnew, cq, ck, wc, tri, *([cache_t] * (2 * pg)), *([logf_t] * pg))


def _softmax_two(s_list, v_list, sn, vn):
    m = jnp.max(sn, axis=1, keepdims=True)
    for s in s_list:
        m = jnp.maximum(m, jnp.max(s, axis=1, keepdims=True))
    pn = jnp.exp(sn - m)
    den = jnp.sum(pn, axis=1, keepdims=True)
    out = jnp.zeros((sn.shape[0], vn.shape[1]), f32)
    for c in range(vn.shape[0]):
        out = out + pn[:, c:c + 1] * vn[c:c + 1, :]
    for s, v in zip(s_list, v_list):
        p = jnp.exp(s - m)
        den = den + jnp.sum(p, axis=1, keepdims=True)
        out = out + _nt_dot(p.astype(bf16), v)
    return out / den


def _new_token_scores(qg, kn, slope_col, t):
    tcol = lax.broadcasted_iota(jnp.int32, (1, SUBLANES), 1)
    sn = jnp.zeros((SUBLANES, SUBLANES), f32)
    for c in range(SUBLANES):
        sn = jnp.where(tcol == c, jnp.sum(qg * kn[c:c + 1, :], axis=1, keepdims=True), sn)
    return jnp.where(tcol <= t, sn - slope_col * (t - tcol).astype(f32), NEG)


def _nsa_decode_kernel(idx_ref, pt_ref, qs_ref, ksn_ref, vsn_ref, kwn_ref, vwn_ref, win_ref, *rest, past):
    n_k = N_KV_NSA * TOP_N
    k_refs, v_refs = rest[:n_k], rest[n_k:2 * n_k]
    os_ref, ow_ref = rest[2 * n_k:]
    b, t = pl.program_id(0), pl.program_id(1)
    qpos = past + t
    lane = lax.broadcasted_iota(jnp.int32, (1, PAGE), 1)
    lane_half = lax.shift_right_logical(lane, 6)
    rows = lax.broadcasted_iota(jnp.int32, (SUBLANES, 1), 0)
    n_past_blocks = past // SLC_BLOCK
    wcol = lax.broadcasted_iota(jnp.int32, (1, WINDOW), 1)
    for g in range(N_KV_NSA):
        slope_col = jnp.zeros((SUBLANES, 1), f32)
        for r in range(HEADS_PER_GROUP):
            slope_col = jnp.where(rows == r, _alibi_slope(g * HEADS_PER_GROUP + r), slope_col)
        qg = qs_ref[0, 0, g] * 0.125
        qgb = qg.astype(bf16)
        s_list, v_list = [], []
        for k in range(TOP_N):
            blk = idx_ref[b, t, g * TOP_N + k]
            half = lax.rem(blk, 2)
            kpos = (blk // 2) * PAGE + lane
            ok = jnp.logical_and(lane_half == half, blk < n_past_blocks)
            s = jnp.dot(qgb, k_refs[g * TOP_N + k][0, 0, 0].astype(bf16), preferred_element_type=f32)
            s_list.append(jnp.where(ok, s - slope_col * (qpos - kpos).astype(f32), NEG))
            v_list.append(v_refs[g * TOP_N + k][0, 0, 0].astype(bf16))
        sn = _new_token_scores(qg, ksn_ref[0, g], slope_col, t)
        os_ref[0, 0, g] = _softmax_two(s_list, v_list, sn, vsn_ref[0, g])
        d = WINDOW + t - wcol
        sw = jnp.dot(qgb, win_ref[0, 0, g].astype(bf16), preferred_element_type=f32)
        sw = jnp.where(d <= WINDOW, sw - slope_col * d.astype(f32), NEG)
        swn = _new_token_scores(qg, kwn_ref[0, g], slope_col, t)
        ow_ref[0, 0, g] = _softmax_two([sw], [win_ref[0, 0, N_KV_NSA + g].astype(bf16)], swn, vwn_ref[0, g])


def _nsa_decode(l, idx, page_table, qs, ksn, vsn, kwn, vwn, win_t, cache_t, past):
    n, t_new = idx.shape[0], idx.shape[1]
    n_pages = page_table.shape[1]

    def sel_map(g, k, row0):
        def f(b, t, idx_r, pt_r):
            page = jnp.minimum(idx_r[b, t, g * TOP_N + k] // 2, n_pages - 1)
            return (l, pt_r[b, page], row0 + g, 0, 0)
        return f

    tile = (1, 1, 1, HEAD_DIM, PAGE)
    k_specs = [pl.BlockSpec(tile, sel_map(g, k, ROW_SLC)) for g in range(N_KV_NSA) for k in range(TOP_N)]
    v_specs = [pl.BlockSpec(tile, sel_map(g, k, ROW_SLC + N_KV_NSA)) for g in range(N_KV_NSA) for k in range(TOP_N)]
    new_spec = pl.BlockSpec((1, N_KV_NSA, SUBLANES, HEAD_DIM), lambda b, t, i, p: (b, 0, 0, 0))
    q_spec = pl.BlockSpec((1, 1, N_KV_NSA, SUBLANES, HEAD_DIM), lambda b, t, i, p: (b, t, 0, 0, 0))
    grid_spec = pltpu.PrefetchScalarGridSpec(
        num_scalar_prefetch=2,
        grid=(n, t_new),
        in_specs=[q_spec, new_spec, new_spec, new_spec, new_spec,
                  pl.BlockSpec((1, 1, 2 * N_KV_NSA, HEAD_DIM, WINDOW), lambda b, t, i, p: (l, b, 0, 0, 0))] + k_specs + v_specs,
        out_specs=[q_spec, q_spec])
    n_k = N_KV_NSA * TOP_N
    shp = jax.ShapeDtypeStruct((n, t_new, N_KV_NSA, SUBLANES, HEAD_DIM), f32)
    return pl.pallas_call(
        functools.partial(_nsa_decode_kernel, past=past),
        grid_spec=grid_spec,
        out_shape=[shp, shp],
        compiler_params=_cparams(("parallel", "arbitrary")),
        name="decode_slc_win",
    )(idx, page_table, qs, ksn, vsn, kwn, vwn, win_t, *([cache_t] * (2 * n_k)))


def _conv_sample_kernel(b_ref, c_ref, u_ref, w_ref, o_ref, un_ref):
    u = c_ref[...] * u_ref[...]
    w = w_ref[...]
    y = w[0:1] * pltpu.roll(u, 2, 0) + w[1:2] * pltpu.roll(u, 1, 0) + w[2:3] * u
    o_ref[...] = b_ref[...] * y
    un_ref[...] = u


def _conv_sample(b_ext, c_ext, u_ext, w8):
    shp = jax.ShapeDtypeStruct(b_ext.shape, f32)
    return pl.pallas_call(_conv_sample_kernel, out_shape=[shp, shp], name="conv_sample")(b_ext, c_ext, u_ext, w8)


def _sample_mixer(x, lw, l, cache_t, logf_t, win_t, conv_state, page_table, past, n, t_new, pg):
    pk = lw["pk"]
    m = n * t_new
    assert t_new * H_FOX == Q_ROWS and m == LANES and past % PAGE == 0 and past >= WINDOW
    pr, pt = _project(x, pk["wr_all"], pk["wt"], m)
    lf_t, cum_t = _logf_cumsum(pt, pk["bf8"], m, segment=t_new)
    seq = lambda a: a.reshape(n, t_new, -1)
    pad8 = lambda a: jnp.pad(a, ((0, 0), (0, SUBLANES - t_new), (0, 0)))
    kv_rows = pr[:, R_KV:R_WIN]
    win_rows = pr[:, R_WIN:R_ALL_END]
    head_mask = jnp.asarray((np.arange(FOX_W)[None, :] // HEAD_DIM == np.arange(H_FOX)[:, None]).astype(np.float32))
    qbd = (seq(pr[:, R_QF:R_QF + FOX_W])[:, :, None, :] * head_mask[None, None]).reshape(n, Q_ROWS, FOX_W)
    knew, vnew = pad8(seq(kv_rows[:, 0:FOX_W])), pad8(seq(kv_rows[:, FOX_W:2 * FOX_W]))
    cum = cum_t[:H_FOX].reshape(H_FOX, n, t_new)
    cq = jnp.transpose(cum, (1, 2, 0)).reshape(n, Q_ROWS, 1)
    ck = jnp.broadcast_to(jnp.transpose(cum, (1, 0, 2))[:, None], (n, t_new, H_FOX, t_new)).reshape(n, Q_ROWS, t_new)
    ck = jnp.pad(ck, ((0, 0), (0, 0), (0, SUBLANES - t_new)))
    o16, hh = _decode(l, page_table, qbd, knew, vnew, cq, ck, pk["wc"], cache_t, logf_t, pg)
    o16 = o16.reshape(n, t_new, H_FOX, H_FOX, HEAD_DIM)
    o_fox = jnp.stack([o16[:, :, h, h] for h in range(H_FOX)], axis=2).reshape(m, FOX_W)
    qn = pr[:, R_QN:R_QN + H_NSA * HEAD_DIM]
    q8 = pad8(seq(qn)).reshape(n * SUBLANES, H_NSA * HEAD_DIM)
    o_c8, _, idx8 = _cmp_select(q8, 0, hh, past, t_new, SUBLANES)
    o_c = o_c8.reshape(n, SUBLANES, -1)[:, :t_new].reshape(m, -1)
    idx = idx8.reshape(n, SUBLANES, LANES)[:, :t_new, :N_KV_NSA * TOP_N]
    qs = qn.reshape(n, t_new, N_KV_NSA, HEADS_PER_GROUP, HEAD_DIM)
    qs = jnp.pad(qs, ((0, 0), (0, 0), (0, 0), (0, SUBLANES - HEADS_PER_GROUP), (0, 0)))
    grp = lambda a: jnp.pad(jnp.transpose(a.reshape(n, t_new, N_KV_NSA, HEAD_DIM), (0, 2, 1, 3)),
                            ((0, 0), (0, 0), (0, SUBLANES - t_new), (0, 0)))
    gw = N_KV_NSA * HEAD_DIM
    ksn, vsn = grp(kv_rows[:, 6 * gw:7 * gw]), grp(kv_rows[:, 7 * gw:8 * gw])
    kwn, vwn = grp(win_rows[:, 0:gw]), grp(win_rows[:, gw:2 * gw])
    o_s5, o_w5 = _nsa_decode(l, idx, page_table, qs, ksn, vsn, kwn, vwn, win_t, cache_t, past)
    heads = lambda a: a[:, :, :, :HEADS_PER_GROUP].reshape(m, H_NSA * HEAD_DIM)
    o_s, o_w = heads(o_s5), heads(o_w5)
    ext = lambda a, head: jnp.concatenate([head, seq(a)], axis=1).reshape(n * SUBLANES, D_CONV)
    zeros2 = jnp.zeros((n, 2, D_CONV), f32)
    one_head = jnp.concatenate([zeros2, jnp.ones((n, 2, D_CONV), f32)], axis=1)
    b_ext = ext(pr[:, R_BG:R_BG + D_CONV], jnp.zeros((n, 4, D_CONV), f32))
    c_ext = ext(pr[:, R_CG:R_CG + D_CONV], one_head)
    u_ext = ext(pr[:, R_UIN:R_UIN + D_CONV], jnp.concatenate([zeros2, conv_state], axis=1))
    o_conv8, u8 = _conv_sample(b_ext, c_ext, u_ext, pk["w8"])
    o_conv = o_conv8.reshape(n, SUBLANES, D_CONV)[:, SUBLANES - t_new:].reshape(m, D_CONV)
    new_conv = u8.reshape(n, SUBLANES, D_CONV)[:, SUBLANES - (CONV_K - 1):]
    new_kv = kv_rows.reshape(n, t_new, N_KV_FEAT // HEAD_DIM, HEAD_DIM)
    new_logf = jnp.transpose(lf_t[:H_FOX].reshape(H_FOX, n, t_new), (1, 2, 0))
    win_new_t = jnp.transpose(pt[T_KW:T_FL].reshape(2 * N_KV_NSA, HEAD_DIM, n, t_new), (2, 0, 1, 3))
    new_win_t = jnp.concatenate([win_t[l][..., t_new:], win_new_t], axis=-1)
    return pr, o_fox, o_conv, o_c, o_s, o_w, new_kv, new_logf, new_conv, new_win_t


def _trunk_sample(x, layers, alpha, cache_t, logf_t, win_t, state_conv, page_table, tiles):
    n, t_new, d = x.shape
    past = page_table.shape[1] * PAGE
    x = x.reshape(n * t_new, d)
    m = n * t_new
    pg = min(tiles["pg"], page_table.shape[1])
    kv, logf, conv, win = [], [], [], []
    for l, lw in enumerate(layers):
        pr, o_fox, o_conv, o_c, o_s, o_w, nkv, nlogf, nconv, nwin_t = _sample_mixer(
            x, lw, l, cache_t, logf_t, win_t, state_conv[l], page_table, past, n, t_new, pg)
        x = _outproj_ln(x, o_fox, o_conv, o_c, o_s, o_w, pr, R_SM, lw["w_out"], lw["ln1_g"], lw["ln1_b"], alpha, m)
        if lw["moe"] is None:
            w1, w3, w2 = lw["dense"]
            x = _ffn_ln(x, w1, w3, w2, lw["ln2_g"], lw["ln2_b"], alpha, m, tiles["tf"])
        else:
            rw, w1, w3, w2 = lw["moe"]
            x = _moe_ln(x, rw, w1, w3, w2, lw["ln2_g"], lw["ln2_b"], alpha, m, tiles["tf"])
        kv.append(nkv)
        logf.append(nlogf)
        conv.append(nconv)
        win.append(jnp.transpose(nwin_t, (0, 3, 1, 2)))
    return x.reshape(n, t_new, d), jnp.stack(kv), jnp.stack(logf), jnp.stack(conv), jnp.stack(win)


def kernel(x_prompt, x_sample, cache_kv, cache_logf, state_conv, state_win, page_table, w_in, b_f, conv_w, w_cmp_k, w_cmp_v, w_out, ln1_g, ln1_b, ln2_g, ln2_b, dense_w1, dense_w3, dense_w2, router_w, moe_w1, moe_w3, moe_w2):
    depth = w_in.shape[0]
    alpha = (2.0 * depth) ** 0.25
    layers = _prepare_layers(w_in, b_f, conv_w, w_cmp_k, w_cmp_v, w_out, ln1_g, ln1_b, ln2_g, ln2_b,
                             dense_w1, dense_w3, dense_w2, router_w, moe_w1, moe_w3, moe_w2)
    y_p, kv_p, logf_p, conv_p, win_p = _trunk_prompt(x_prompt[0], layers, alpha, TILES)
    cache_t = jnp.transpose(cache_kv, (0, 1, 3, 4, 2))
    logf_t = jnp.transpose(cache_logf, (0, 1, 3, 2))
    win_t = jnp.transpose(state_win, (0, 1, 3, 4, 2))
    y_s, kv_s, logf_s, conv_s, win_s = _trunk_sample(
        x_sample, layers, alpha, cache_t, logf_t, win_t, state_conv, page_table, TILES)
    return (y_p[None], y_s, kv_p, logf_p, conv_p, win_p, kv_s, logf_s, conv_s, win_s)
```

```python
import functools

import numpy as np
import jax
import jax.numpy as jnp
from jax import lax
from jax.experimental import pallas as pl
from jax.experimental.pallas import tpu as pltpu

f32 = jnp.float32
bf16 = jnp.bfloat16

HEAD_DIM = 64
H_FOX = 4
D_CONV = 256
H_NSA = 8
N_KV_NSA = 2
HEADS_PER_GROUP = H_NSA // N_KV_NSA
CONV_K = 3
CMP_STRIDE = 16
SLC_BLOCK = 64
TOP_N = 16
N_LOCAL_BLOCKS = 2
SLC_COVER_W = (1.0, 2.0, 2.0, 2.0, 1.0)
WINDOW = 512
N_EXPERTS = 8
TOP_K = 2
LN_EPS = 1e-5
FORCE_BONUS = 1e4
NEG = -1e30
LANES = 128
SUBLANES = 8
VMEM_LIMIT = 56 * 1024 * 1024
HIGHEST = lax.Precision.HIGHEST

_C_QF, _C_KF, _C_VF, _C_FL = 0, 256, 512, 768
_C_BG, _C_CG, _C_UIN, _C_QN = 772, 1028, 1284, 1540
_C_KC, _C_VC, _C_KS, _C_VS, _C_KW, _C_VW, _C_GN, _C_END = 2052, 2180, 2308, 2436, 2564, 2692, 2820, 2844
R_QN, R_QF, R_BG, R_CG, R_UIN, R_SM, R_END = 0, 512, 768, 1024, 1280, 1536, 1664
SM_GN = 4
T_KF, T_VF, T_KC, T_VC, T_KS, T_VS, T_KW, T_VW, T_FL, T_END = 0, 256, 512, 640, 768, 896, 1024, 1152, 1280, 1288
N_KV_FEAT = 1024


def _cparams(sem):
    return pltpu.CompilerParams(dimension_semantics=sem, vmem_limit_bytes=VMEM_LIMIT)


def _nt_dot(a, b):
    return lax.dot_general(a, b, (((1,), (1,)), ((), ())), preferred_element_type=f32)


def _log_sigmoid(x):
    return -(jnp.maximum(-x, 0.0) + jnp.log1p(jnp.exp(-jnp.abs(x))))


def _sigmoid(x):
    return 1.0 / (1.0 + jnp.exp(-x))


def _proj_kernel(x_ref, wr_ref, wt_ref, pr_ref, pt_ref):
    xb = x_ref[...].astype(bf16)
    pr_ref[...] = jnp.dot(xb, wr_ref[...], preferred_element_type=f32)
    pt_ref[...] = _nt_dot(wt_ref[...], xb)


def _project(x, wr, wt, tm):
    m, d = x.shape
    cr, ct = wr.shape[1], wt.shape[0]
    return pl.pallas_call(
        _proj_kernel,
        grid=(m // tm,),
        in_specs=[pl.BlockSpec((tm, d), lambda i: (i, 0)),
                  pl.BlockSpec((d, cr), lambda i: (0, 0)),
                  pl.BlockSpec((ct, d), lambda i: (0, 0))],
        out_specs=[pl.BlockSpec((tm, cr), lambda i: (i, 0)),
                   pl.BlockSpec((ct, tm), lambda i: (0, i))],
        out_shape=[jax.ShapeDtypeStruct((m, cr), f32), jax.ShapeDtypeStruct((ct, m), f32)],
        compiler_params=_cparams(("parallel",)),
        name="project",
    )(x, wr, wt)


def _logf_cumsum_kernel(f_ref, bf_ref, tri_ref, logf_ref, c_ref, carry_ref):
    @pl.when(pl.program_id(0) == 0)
    def _():
        carry_ref[...] = jnp.zeros_like(carry_ref)

    carry = carry_ref[...]
    n_sub = f_ref.shape[1] // LANES
    for s in range(n_sub):
        sl = slice(s * LANES, (s + 1) * LANES)
        lf = _log_sigmoid(f_ref[:, sl] + bf_ref[...])
        logf_ref[:, sl] = lf
        cs = jnp.dot(lf, tri_ref[...], precision=HIGHEST, preferred_element_type=f32) + carry
        c_ref[:, sl] = cs
        carry = jnp.broadcast_to(cs[:, LANES - 1:LANES], carry.shape)
    carry_ref[...] = carry


def _logf_cumsum(pt, bf8, tc, segment=LANES):
    t = pt.shape[1]
    ii, jj = np.meshgrid(np.arange(LANES), np.arange(LANES), indexing="ij")
    tri = jnp.asarray(((ii <= jj) & (ii // segment == jj // segment)).astype(np.float32))
    return pl.pallas_call(
        _logf_cumsum_kernel,
        grid=(t // tc,),
        in_specs=[pl.BlockSpec((SUBLANES, tc), lambda j: (T_FL // SUBLANES, j)),
                  pl.BlockSpec((SUBLANES, 1), lambda j: (0, 0)),
                  pl.BlockSpec((LANES, LANES), lambda j: (0, 0))],
        out_specs=[pl.BlockSpec((SUBLANES, tc), lambda j: (0, j)),
                   pl.BlockSpec((SUBLANES, tc), lambda j: (0, j))],
        out_shape=[jax.ShapeDtypeStruct((SUBLANES, t), f32)] * 2,
        scratch_shapes=[pltpu.VMEM((SUBLANES, LANES), f32)],
        compiler_params=_cparams(("arbitrary",)),
        name="logf_cumsum",
    )(pt, bf8, tri)


def _alibi_slope(h):
    return float(2.0 ** (-8.0 * (h + 1) / H_NSA))


def _flash_kernel(*refs, mode, n_heads, tq, tk):
    if mode == "fox":
        q_ref, kt_ref, vt_ref, c_ref, cq_ref, bnd_ref, o_ref, m_ref, l_ref, acc_ref = refs
    elif mode == "slc":
        flags_ref, q_ref, kt_ref, vt_ref, sel_ref, o_ref, m_ref, l_ref, acc_ref = refs
    else:
        q_ref, kt_ref, vt_ref, o_ref, m_ref, l_ref, acc_ref = refs
    i, j = pl.program_id(0), pl.program_id(1)
    nq, nk = pl.num_programs(0), pl.num_programs(1)

    @pl.when(j == 0)
    def _():
        m_ref[...] = jnp.full_like(m_ref, NEG)
        l_ref[...] = jnp.zeros_like(l_ref)
        acc_ref[...] = jnp.zeros_like(acc_ref)

    if mode == "win":
        kv_tile = jnp.maximum(i - 1 + j, 0)
        active = jnp.logical_or(i > 0, j > 0)
    elif mode == "fox":
        kv_tile = jnp.maximum(i - j, 0)
        active = j <= i
    else:
        kv_tile = j
        active = j <= i
    q0 = i * tq

    def positions():
        qpos = q0 + lax.broadcasted_iota(jnp.int32, (tq, 1), 0)
        kpos = kv_tile * tk + lax.broadcasted_iota(jnp.int32, (1, tk), 1)
        return qpos, kpos

    def attend(h, kv, bias, mask):
        qh = (q_ref[:, h * HEAD_DIM:(h + 1) * HEAD_DIM] * 0.125).astype(bf16)
        kth = kt_ref[kv * HEAD_DIM:(kv + 1) * HEAD_DIM, :].astype(bf16)
        vth = vt_ref[kv * HEAD_DIM:(kv + 1) * HEAD_DIM, :].astype(bf16)
        s = jnp.dot(qh, kth, preferred_element_type=f32) + bias + mask
        m_old = m_ref[h]
        m_new = jnp.maximum(m_old, jnp.max(s, axis=1, keepdims=True))
        alpha = jnp.exp(m_old - m_new)
        p = jnp.exp(s - m_new)
        l_ref[h] = alpha * l_ref[h] + jnp.sum(p, axis=1, keepdims=True)
        acc_ref[h] = alpha * acc_ref[h] + _nt_dot(p.astype(bf16), vth)
        m_ref[h] = m_new

    if mode == "fox":
        dead = jnp.ones((tq, 1), f32)
        for h in range(n_heads):
            dead = dead * jnp.where(m_ref[h] > bnd_ref[0, 0, h:h + 1, 0:1], 1.0, 0.0)
        live = jnp.min(dead) < 0.5

        @pl.when(jnp.logical_and(active, live))
        def _():
            qpos, kpos = positions()
            neg = jnp.where(kpos > qpos, NEG, 0.0)
            for h in range(n_heads):
                attend(h, h, cq_ref[h:h + 1, 0:1] - c_ref[h:h + 1, :], neg)
    elif mode == "slc":
        nbs = sel_ref.shape[1] // N_KV_NSA
        for g in range(N_KV_NSA):
            @pl.when(jnp.logical_and(active, flags_ref[(g * nq + i) * nk + kv_tile] > 0))
            def _(g=g):
                qpos, kpos = positions()
                rel = (kpos - q0).astype(f32)
                blk_of_key = lax.shift_right_logical(kpos, 6)
                expand = (lax.broadcasted_iota(jnp.int32, (nbs, 1), 0) == blk_of_key).astype(bf16)
                selg = sel_ref[:, g * nbs:(g + 1) * nbs].astype(bf16)
                hit = jnp.dot(selg, expand, preferred_element_type=f32)
                mask = jnp.where(jnp.logical_and(hit > 0.5, kpos <= qpos), 0.0, NEG)
                for r in range(HEADS_PER_GROUP):
                    h = g * HEADS_PER_GROUP + r
                    attend(h, g, _alibi_slope(h) * rel, mask)
    else:
        @pl.when(active)
        def _():
            qpos, kpos = positions()
            d = qpos - kpos
            neg = jnp.where(jnp.logical_or(d < 0, d > WINDOW), NEG, 0.0)
            rel = (kpos - q0).astype(f32)
            for h in range(n_heads):
                attend(h, h // HEADS_PER_GROUP, _alibi_slope(h) * rel, neg)

    @pl.when(j == nk - 1)
    def _():
        for h in range(n_heads):
            o_ref[:, h * HEAD_DIM:(h + 1) * HEAD_DIM] = acc_ref[h] / l_ref[h]


EXP_DEAD_MARGIN = 110.0


def _fox_tile_bounds(pr, pt, c_t, tq):
    t = pr.shape[0]
    nq = t // tq
    q = pr[:, R_QF:R_QF + H_FOX * HEAD_DIM].reshape(nq, tq, H_FOX, HEAD_DIM)
    qn = jnp.sqrt(jnp.max(jnp.sum(q * q, axis=-1), axis=1))
    k = pt[T_KF:T_KF + H_FOX * HEAD_DIM].reshape(H_FOX, HEAD_DIM, nq, tq)
    kn = jnp.sqrt(jnp.max(jnp.sum(k * k, axis=1), axis=-1)).T
    c4 = c_t[:H_FOX].reshape(H_FOX, nq, tq)
    cq0, cend = c4[:, :, 0].T, c4[:, :, -1].T
    bnd = 1.01 * 0.125 * qn[:, None, :] * kn[None, :, :] + (cq0[:, None, :] - cend[None, :, :]) + EXP_DEAD_MARGIN
    bnd = jnp.pad(bnd, ((0, 0), (0, 0), (0, SUBLANES - H_FOX)))
    return jnp.broadcast_to(bnd[..., None], (nq, nq, SUBLANES, LANES))


def _slc_tile_flags(anyb, tq, tq_cmp, t):
    nq = t // tq
    nbs = anyb.shape[1] // N_KV_NSA
    per_kv = tq // SLC_BLOCK
    a = anyb[::SUBLANES].reshape(nq, tq // tq_cmp, N_KV_NSA, nbs // per_kv, per_kv)
    flags = (jnp.max(a, axis=(1, 4)) > 0.5)[:, :, :nq]
    return jnp.transpose(flags, (1, 0, 2)).reshape(-1).astype(jnp.int32)


def _flash(mode, pr, pt, q_col, kt_row, vt_row, tq, extra=()):
    t = pr.shape[0]
    n_heads = H_FOX if mode == "fox" else H_NSA
    n_kv = H_FOX if mode == "fox" else N_KV_NSA
    qw, kw = n_heads * HEAD_DIM, n_kv * HEAD_DIM
    tk = tq
    nq = t // tq
    nk = 2 if mode == "win" else nq
    if mode == "win":
        kv_map = lambda i, j: jnp.maximum(i - 1 + j, 0)
    elif mode == "fox":
        kv_map = lambda i, j: jnp.maximum(i - j, 0)
    else:
        kv_map = lambda i, j: jnp.minimum(j, i)
    in_specs = [pl.BlockSpec((tq, qw), lambda i, j, *_: (i, q_col // qw)),
                pl.BlockSpec((kw, tk), lambda i, j, *_: (kt_row // kw, kv_map(i, j))),
                pl.BlockSpec((kw, tk), lambda i, j, *_: (vt_row // kw, kv_map(i, j)))]
    prefetch = ()
    if mode == "fox":
        (ct,) = extra
        in_specs += [pl.BlockSpec((SUBLANES, tk), lambda i, j, *_: (0, kv_map(i, j))),
                     pl.BlockSpec((SUBLANES, LANES), lambda i, j, *_: (0, i * (tq // LANES))),
                     pl.BlockSpec((1, 1, SUBLANES, LANES), lambda i, j, *_: (i, kv_map(i, j), 0, 0))]
        args = (pr, pt, pt, ct, ct, _fox_tile_bounds(pr, pt, ct, tq))
    elif mode == "slc":
        sel, flags = extra
        in_specs += [pl.BlockSpec((tq, sel.shape[1]), lambda i, j, *_: (i, 0))]
        args = (pr, pt, pt, sel)
        prefetch = (flags,)
    else:
        args = (pr, pt, pt)
    grid_spec = pltpu.PrefetchScalarGridSpec(
        num_scalar_prefetch=len(prefetch),
        grid=(nq, nk),
        in_specs=in_specs,
        out_specs=pl.BlockSpec((tq, qw), lambda i, j, *_: (i, 0)),
        scratch_shapes=[pltpu.VMEM((n_heads, tq, 1), f32), pltpu.VMEM((n_heads, tq, 1), f32),
                        pltpu.VMEM((n_heads, tq, HEAD_DIM), f32)])
    return pl.pallas_call(
        functools.partial(_flash_kernel, mode=mode, n_heads=n_heads, tq=tq, tk=tk),
        grid_spec=grid_spec,
        out_shape=jax.ShapeDtypeStruct((t, qw), f32),
        compiler_params=_cparams(("parallel", "arbitrary")),
        name="flash_" + mode,
    )(*prefetch, *args)


def _compress_tile(a, at_ref, wc):
    tb = a.shape[1]
    at_ref[...] = a.T
    n_half = tb // CMP_STRIDE
    acc = jnp.zeros((n_half, 2 * HEAD_DIM), f32)
    for r in range(CMP_STRIDE):
        rows = at_ref[pl.ds(r, n_half, stride=CMP_STRIDE), :]
        acc = acc + jnp.dot(rows.astype(bf16), wc(r), preferred_element_type=f32)
    return acc


def _compress_kernel(a_ref, wc_ref, o_ref, at_ref):
    o_ref[0] = _compress_tile(a_ref[...], at_ref, lambda r: wc_ref[0, r])


def _compress(pt, wc, tb):
    t = pt.shape[1]
    base = T_KC // HEAD_DIM
    return pl.pallas_call(
        _compress_kernel,
        grid=(2 * N_KV_NSA, t // tb),
        in_specs=[pl.BlockSpec((HEAD_DIM, tb), lambda k, j: (base + k, j)),
                  pl.BlockSpec((1, CMP_STRIDE, HEAD_DIM, 2 * HEAD_DIM), lambda k, j: (k, 0, 0, 0))],
        out_specs=pl.BlockSpec((1, tb // CMP_STRIDE, 2 * HEAD_DIM), lambda k, j: (k, j, 0)),
        out_shape=jax.ShapeDtypeStruct((2 * N_KV_NSA, t // CMP_STRIDE, 2 * HEAD_DIM), f32),
        scratch_shapes=[pltpu.VMEM((tb, HEAD_DIM), f32)],
        compiler_params=_cparams(("parallel", "parallel")),
        name="compress",
    )(pt, wc)


def _cmp_select_kernel(q_ref, hh_ref, cov_ref, oc_ref, sel_ref, idx_ref, any_ref, *, past, tq, nbs):
    n_half = hh_ref.shape[2]
    nbsp = cov_ref.shape[1]
    t0 = pl.program_id(1) * tq
    qpos = past + t0 + lax.broadcasted_iota(jnp.int32, (tq, 1), 0)
    cmp_end = lax.broadcasted_iota(jnp.int32, (1, n_half), 1) * CMP_STRIDE + (2 * CMP_STRIDE - 1)
    valid_c = cmp_end <= qpos
    dist = (qpos - cmp_end).astype(f32)
    blk = lax.broadcasted_iota(jnp.int32, (1, nbsp), 1)
    blk_f = blk.astype(f32)
    cur = lax.shift_right_logical(qpos, 6)
    valid_s = blk <= cur
    forced = jnp.logical_or(blk == 0, blk >= cur - (N_LOCAL_BLOCKS - 1))
    lane = lax.broadcasted_iota(jnp.int32, (1, LANES), 1)
    idx_acc = jnp.zeros((tq, LANES), f32)
    for g in range(N_KV_NSA):
        hk = hh_ref[0, g]
        hv = hh_ref[0, N_KV_NSA + g]
        ck = hk[:, :HEAD_DIM] + pltpu.roll(hk, n_half - 1, 0)[:, HEAD_DIM:]
        cv = hv[:, :HEAD_DIM] + pltpu.roll(hv, n_half - 1, 0)[:, HEAD_DIM:]
        ckb, cvb = ck.astype(bf16), cv.astype(bf16)
        imp = jnp.zeros((tq, n_half), f32)
        for r in range(HEADS_PER_GROUP):
            h = g * HEADS_PER_GROUP + r
            qh = (q_ref[:, h * HEAD_DIM:(h + 1) * HEAD_DIM] * 0.125).astype(bf16)
            s = _nt_dot(qh, ckb) - _alibi_slope(h) * dist
            s = jnp.where(valid_c, s, -jnp.inf)
            m = jnp.max(s, axis=1, keepdims=True)
            m = jnp.where(m == -jnp.inf, 0.0, m)
            e = jnp.exp(s - m)
            p = e / jnp.maximum(jnp.sum(e, axis=1, keepdims=True), 1e-30)
            oc_ref[:, h * HEAD_DIM:(h + 1) * HEAD_DIM] = jnp.dot(p.astype(bf16), cvb, preferred_element_type=f32)
            imp = imp + p
        score = jnp.dot(imp, cov_ref[...], precision=HIGHEST, preferred_element_type=f32)
        score = jnp.where(valid_s, score + jnp.where(forced, FORCE_BONUS, 0.0), -FORCE_BONUS)
        score = jnp.where(blk < nbs, score, -jnp.inf)
        selm = jnp.zeros((tq, nbsp), f32)
        for k in range(min(TOP_N, nbs)):
            mx = jnp.max(score, axis=1, keepdims=True)
            ix = jnp.min(jnp.where(score == mx, blk_f, 1e9), axis=1, keepdims=True)
            hit = blk_f == ix
            selm = jnp.where(hit, 1.0, selm)
            score = jnp.where(hit, -jnp.inf, score)
            idx_acc = jnp.where(lane == g * TOP_N + k, ix, idx_acc)
        sel_ref[:, g * nbsp:(g + 1) * nbsp] = selm
        any_ref[:, g * nbsp:(g + 1) * nbsp] = jnp.broadcast_to(jnp.max(selm, axis=0, keepdims=True), (SUBLANES, nbsp))
    idx_ref[...] = idx_acc.astype(jnp.int32)


def _cover_matrix(n_half, nbs, nbsp):
    cov = np.zeros((n_half, nbsp), np.float32)
    ratio = SLC_BLOCK // CMP_STRIDE
    for jb in range(nbs):
        for mm, w in enumerate(SLC_COVER_W):
            c = ratio * jb + mm - 1
            if 0 <= c < n_half - 1:
                cov[c, jb] += w
    return jnp.asarray(cov)


def _cmp_select(q, q_col, hh, past, t_len, tq):
    hh = hh.reshape(-1, 2 * N_KV_NSA, hh.shape[-2], hh.shape[-1])
    n = hh.shape[0]
    n_half = hh.shape[2]
    tp = q.shape[0] // n
    nbs = -(-(past + t_len) // SLC_BLOCK)
    nbsp = -(-nbs // LANES) * LANES
    cov = _cover_matrix(n_half, nbs, nbsp)
    qw = H_NSA * HEAD_DIM
    hh4 = hh
    nt = tp // tq
    return pl.pallas_call(
        functools.partial(_cmp_select_kernel, past=past, tq=tq, nbs=nbs),
        grid=(n, nt),
        in_specs=[pl.BlockSpec((tq, qw), lambda b, i: (b * nt + i, q_col // qw)),
                  pl.BlockSpec((1, 2 * N_KV_NSA, n_half, 2 * HEAD_DIM), lambda b, i: (b, 0, 0, 0)),
                  pl.BlockSpec((n_half, nbsp), lambda b, i: (0, 0))],
        out_specs=[pl.BlockSpec((tq, qw), lambda b, i: (b * nt + i, 0)),
                   pl.BlockSpec((tq, N_KV_NSA * nbsp), lambda b, i: (b * nt + i, 0)),
                   pl.BlockSpec((tq, LANES), lambda b, i: (b * nt + i, 0)),
                   pl.BlockSpec((SUBLANES, N_KV_NSA * nbsp), lambda b, i: (b * nt + i, 0))],
        out_shape=[jax.ShapeDtypeStruct((n * tp, qw), f32),
                   jax.ShapeDtypeStruct((n * tp, N_KV_NSA * nbsp), f32),
                   jax.ShapeDtypeStruct((n * tp, LANES), jnp.int32),
                   jax.ShapeDtypeStruct((n * nt * SUBLANES, N_KV_NSA * nbsp), f32)],
        compiler_params=_cparams(("parallel", "parallel")),
        name="cmp_select",
    )(q, hh4, cov)


def _conv_kernel(b_ref, c_ref, u_ref, ch_ref, uh_ref, st_ref, w_ref, o_ref, last_ref):
    i = pl.program_id(0)
    u = c_ref[...] * u_ref[...]
    halo = jnp.where(i == 0, st_ref[...], ch_ref[...] * uh_ref[...])
    ue = jnp.concatenate([halo, u], axis=0)
    u1 = pltpu.roll(ue, 1, 0)[SUBLANES:]
    u2 = pltpu.roll(ue, 2, 0)[SUBLANES:]
    w = w_ref[...]
    y = w[0:1] * u2 + w[1:2] * u1 + w[2:3] * u
    o_ref[...] = b_ref[...] * y
    last_ref[...] = u[u.shape[0] - SUBLANES:]


def _conv_prompt(pr, state8, w8, tm):
    t = pr.shape[0]
    cb = lambda off: off // D_CONV
    hb = tm // SUBLANES
    return pl.pallas_call(
        _conv_kernel,
        grid=(t // tm,),
        in_specs=[pl.BlockSpec((tm, D_CONV), lambda i: (i, cb(R_BG))),
                  pl.BlockSpec((tm, D_CONV), lambda i: (i, cb(R_CG))),
                  pl.BlockSpec((tm, D_CONV), lambda i: (i, cb(R_UIN))),
                  pl.BlockSpec((SUBLANES, D_CONV), lambda i: (jnp.maximum(i * hb - 1, 0), cb(R_CG))),
                  pl.BlockSpec((SUBLANES, D_CONV), lambda i: (jnp.maximum(i * hb - 1, 0), cb(R_UIN))),
                  pl.BlockSpec((SUBLANES, D_CONV), lambda i: (0, 0)),
                  pl.BlockSpec((SUBLANES, D_CONV), lambda i: (0, 0))],
        out_specs=[pl.BlockSpec((tm, D_CONV), lambda i: (i, 0)),
                   pl.BlockSpec((SUBLANES, D_CONV), lambda i: (0, 0))],
        out_shape=[jax.ShapeDtypeStruct((t, D_CONV), f32), jax.ShapeDtypeStruct((SUBLANES, D_CONV), f32)],
        compiler_params=_cparams(("arbitrary",)),
        name="conv",
    )(pr, pr, pr, pr, pr, state8, w8)


def _layernorm(z, g, b):
    mu = jnp.mean(z, axis=-1, keepdims=True)
    zc = z - mu
    var = jnp.mean(zc * zc, axis=-1, keepdims=True)
    return zc * lax.rsqrt(var + LN_EPS) * g + b


def _outproj_kernel(x_ref, fox_ref, conv_ref, oc_ref, os_ref, ow_ref, sm_ref, ex_ref, w_ref, g_ref, b_ref,
                    o_ref, *, alpha):
    sig = _sigmoid(sm_ref[...])
    nsa = None
    for br, br_ref in enumerate((oc_ref, os_ref, ow_ref)):
        gate = jnp.dot(sig, ex_ref[br], precision=HIGHEST, preferred_element_type=f32)
        term = gate * br_ref[...]
        nsa = term if nsa is None else nsa + term
    a0, a1 = H_FOX * HEAD_DIM, H_FOX * HEAD_DIM + D_CONV
    mix = jnp.dot(fox_ref[...].astype(bf16), w_ref[0:a0, :], preferred_element_type=f32)
    mix = mix + jnp.dot(conv_ref[...].astype(bf16), w_ref[a0:a1, :], preferred_element_type=f32)
    mix = mix + jnp.dot(nsa.astype(bf16), w_ref[a1:, :], preferred_element_type=f32)
    o_ref[...] = _layernorm(alpha * x_ref[...] + mix, g_ref[...], b_ref[...])


def _gate_expand():
    ex = np.zeros((3, LANES, H_NSA * HEAD_DIM), np.float32)
    for h in range(H_NSA):
        for br in range(3):
            ex[br, SM_GN + h * 3 + br, h * HEAD_DIM:(h + 1) * HEAD_DIM] = 1.0
    return jnp.asarray(ex)


def _outproj_ln(x, o_fox, o_conv, o_c, o_s, o_w, small, sm_col, w_out, g, b, alpha, tm):
    m, d = x.shape
    row = lambda w: pl.BlockSpec((tm, w), lambda i: (i, 0))
    full = lambda shp: pl.BlockSpec(shp, lambda i: (0,) * len(shp))
    nw = H_NSA * HEAD_DIM
    return pl.pallas_call(
        functools.partial(_outproj_kernel, alpha=alpha),
        grid=(m // tm,),
        in_specs=[row(d), row(H_FOX * HEAD_DIM), row(D_CONV), row(nw), row(nw), row(nw),
                  pl.BlockSpec((tm, LANES), lambda i: (i, sm_col // LANES)),
                  full((3, LANES, nw)), full(w_out.shape), full((1, d)), full((1, d))],
        out_specs=row(d),
        out_shape=jax.ShapeDtypeStruct((m, d), f32),
        compiler_params=_cparams(("parallel",)),
        name="outproj_ln",
    )(x, o_fox, o_conv, o_c, o_s, o_w, small, _gate_expand(), w_out, g.reshape(1, d), b.reshape(1, d))


def _swiglu_partial(xb, w1, w3, w2):
    h1 = jnp.dot(xb, w1, preferred_element_type=f32)
    h3 = jnp.dot(xb, w3, preferred_element_type=f32)
    hh = (h1 * _sigmoid(h1) * h3).astype(bf16)
    return jnp.dot(hh, w2, preferred_element_type=f32)


def _ffn_kernel(x_ref, w1_ref, w3_ref, w2_ref, g_ref, b_ref, o_ref, acc_ref, *, alpha):
    f = pl.program_id(1)

    @pl.when(f == 0)
    def _():
        acc_ref[...] = jnp.zeros_like(acc_ref)

    acc_ref[...] += _swiglu_partial(x_ref[...].astype(bf16), w1_ref[...], w3_ref[...], w2_ref[...])

    @pl.when(f == pl.num_programs(1) - 1)
    def _():
        o_ref[...] = _layernorm(alpha * x_ref[...] + acc_ref[...], g_ref[...], b_ref[...])


def _ffn_ln(x, w1, w3, w2, g, b, alpha, tm, tf):
    m, d = x.shape
    ff = w1.shape[1]
    return pl.pallas_call(
        functools.partial(_ffn_kernel, alpha=alpha),
        grid=(m // tm, ff // tf),
        in_specs=[pl.BlockSpec((tm, d), lambda i, f: (i, 0)),
                  pl.BlockSpec((d, tf), lambda i, f: (0, f)),
                  pl.BlockSpec((d, tf), lambda i, f: (0, f)),
                  pl.BlockSpec((tf, d), lambda i, f: (f, 0)),
                  pl.BlockSpec((1, d), lambda i, f: (0, 0)),
                  pl.BlockSpec((1, d), lambda i, f: (0, 0))],
        out_specs=pl.BlockSpec((tm, d), lambda i, f: (i, 0)),
        out_shape=jax.ShapeDtypeStruct((m, d), f32),
        scratch_shapes=[pltpu.VMEM((tm, d), f32)],
        compiler_params=_cparams(("parallel", "arbitrary")),
        name="ffn_ln",
    )(x, w1, w3, w2, g.reshape(1, d), b.reshape(1, d))


def _moe_kernel(x_ref, r_ref, w1_ref, w3_ref, w2_ref, g_ref, b_ref, o_ref, acc_ref, gate_ref, *, alpha):
    e, f = pl.program_id(1), pl.program_id(2)
    lane = lax.broadcasted_iota(jnp.int32, (1, LANES), 1)

    @pl.when(jnp.logical_and(e == 0, f == 0))
    def _():
        acc_ref[...] = jnp.zeros_like(acc_ref)
        logits = jnp.dot(x_ref[...], r_ref[...], precision=HIGHEST, preferred_element_type=f32)
        logits = jnp.where(lane < N_EXPERTS, logits, -jnp.inf)
        mx = jnp.max(logits, axis=1, keepdims=True)
        ex = jnp.exp(logits - mx)
        probs = ex / jnp.sum(ex, axis=1, keepdims=True)
        lane_f = lane.astype(f32)
        work = jnp.where(lane < N_EXPERTS, probs, -1.0)
        picked = jnp.zeros_like(probs)
        for _ in range(TOP_K):
            top = jnp.max(work, axis=1, keepdims=True)
            ix = jnp.min(jnp.where(work == top, lane_f, 1e9), axis=1, keepdims=True)
            hit = lane_f == ix
            picked = jnp.where(hit, probs, picked)
            work = jnp.where(hit, -1.0, work)
        gate_ref[...] = picked / jnp.sum(picked, axis=1, keepdims=True)

    part = _swiglu_partial(x_ref[...].astype(bf16), w1_ref[0], w3_ref[0], w2_ref[0])
    gate_e = jnp.sum(jnp.where(lane == e, gate_ref[...], 0.0), axis=1, keepdims=True)
    acc_ref[...] += gate_e * part

    @pl.when(jnp.logical_and(e == pl.num_programs(1) - 1, f == pl.num_programs(2) - 1))
    def _():
        o_ref[...] = _layernorm(alpha * x_ref[...] + acc_ref[...], g_ref[...], b_ref[...])


def _moe_ln(x, router, w1, w3, w2, g, b, alpha, tm, tf):
    m, d = x.shape
    ne, _, ff = w1.shape
    return pl.pallas_call(
        functools.partial(_moe_kernel, alpha=alpha),
        grid=(m // tm, ne, ff // tf),
        in_specs=[pl.BlockSpec((tm, d), lambda i, e, f: (i, 0)),
                  pl.BlockSpec((d, LANES), lambda i, e, f: (0, 0)),
                  pl.BlockSpec((1, d, tf), lambda i, e, f: (e, 0, f)),
                  pl.BlockSpec((1, d, tf), lambda i, e, f: (e, 0, f)),
                  pl.BlockSpec((1, tf, d), lambda i, e, f: (e, f, 0)),
                  pl.BlockSpec((1, d), lambda i, e, f: (0, 0)),
                  pl.BlockSpec((1, d), lambda i, e, f: (0, 0))],
        out_specs=pl.BlockSpec((tm, d), lambda i, e, f: (i, 0)),
        out_shape=jax.ShapeDtypeStruct((m, d), f32),
        scratch_shapes=[pltpu.VMEM((tm, d), f32), pltpu.VMEM((tm, LANES), f32)],
        compiler_params=_cparams(("parallel", "arbitrary", "arbitrary")),
        name="moe_ln",
    )(x, router, w1, w3, w2, g.reshape(1, d), b.reshape(1, d))


def _pack_layer_weights(w_in, b_f, conv_w, w_cmp_k, w_cmp_v):
    d = w_in.shape[0]
    cols = lambda a, b: w_in[:, a:b]
    small = jnp.concatenate([cols(_C_FL, _C_BG), cols(_C_GN, _C_END),
                             jnp.zeros((d, LANES - H_FOX - 3 * H_NSA), w_in.dtype)], axis=1)
    wr = jnp.concatenate([cols(_C_QN, _C_KC), cols(_C_QF, _C_KF), cols(_C_BG, _C_QN), small], axis=1)
    kv_cols = jnp.concatenate([cols(_C_KF, _C_FL), cols(_C_KC, _C_KW)], axis=1)
    win_cols = cols(_C_KW, _C_GN)
    fl = jnp.concatenate([cols(_C_FL, _C_BG), jnp.zeros((d, SUBLANES - H_FOX), w_in.dtype)], axis=1)
    wt = jnp.concatenate([kv_cols, win_cols, fl], axis=1).T
    wr_all = jnp.concatenate([wr, kv_cols, win_cols], axis=1)
    bf8 = jnp.concatenate([b_f, jnp.zeros((SUBLANES - H_FOX,), b_f.dtype)]).reshape(SUBLANES, 1)
    w8 = jnp.concatenate([conv_w, jnp.zeros((SUBLANES - CONV_K, D_CONV), conv_w.dtype)], axis=0)

    def wc_of(w):
        lo = w[:CMP_STRIDE * HEAD_DIM].reshape(CMP_STRIDE, HEAD_DIM, HEAD_DIM)
        hi = w[CMP_STRIDE * HEAD_DIM:].reshape(CMP_STRIDE, HEAD_DIM, HEAD_DIM)
        return jnp.concatenate([lo, hi], axis=-1)

    wck, wcv = wc_of(w_cmp_k), wc_of(w_cmp_v)
    wc = jnp.stack([wck] * N_KV_NSA + [wcv] * N_KV_NSA).astype(bf16)
    return dict(wr=wr.astype(bf16), wr_all=wr_all.astype(bf16), wt=wt.astype(bf16), bf8=bf8, w8=w8, wc=wc)


def _tile(t, want):
    return min(t, want)


def _prompt_mixer(x, pk, w_out_b, tiles):
    t = x.shape[0]
    pr, pt = _project(x, pk["wr"], pk["wt"], _tile(t, tiles["proj"]))
    logf_t, c_t = _logf_cumsum(pt, pk["bf8"], _tile(t, 2048))
    tq = _tile(t, tiles["flash"])
    o_fox = _flash("fox", pr, pt, R_QF, T_KF, T_VF, tq, (c_t,))
    hh = _compress(pt, pk["wc"], _tile(t, 2048))
    tq_cmp = _tile(t, tiles["cmp"])
    o_c, sel, _, anyb = _cmp_select(pr, R_QN, hh, 0, t, tq_cmp)
    o_s = _flash("slc", pr, pt, R_QN, T_KS, T_VS, tq, (sel, _slc_tile_flags(anyb, tq, tq_cmp, t)))
    o_w = _flash("win", pr, pt, R_QN, T_KW, T_VW, _tile(t, WINDOW))
    state8 = jnp.zeros((SUBLANES, D_CONV), f32)
    o_conv, u_last = _conv_prompt(pr, state8, pk["w8"], _tile(t, tiles["proj"]))
    return pr, pt, logf_t, o_fox, o_conv, o_c, o_s, o_w, u_last


def _trunk_prompt(x, layers, alpha, tiles):
    t = x.shape[0]
    kv, logf, conv, win = [], [], [], []
    for lw in layers:
        pr, pt, logf_t, o_fox, o_conv, o_c, o_s, o_w, u_last = _prompt_mixer(x, lw["pk"], lw["w_out"], tiles)
        tm = _tile(t, tiles["row"])
        x = _outproj_ln(x, o_fox, o_conv, o_c, o_s, o_w, pr, R_SM, lw["w_out"], lw["ln1_g"], lw["ln1_b"], alpha, tm)
        if lw["moe"] is None:
            w1, w3, w2 = lw["dense"]
            x = _ffn_ln(x, w1, w3, w2, lw["ln2_g"], lw["ln2_b"], alpha, _tile(t, tiles["ffn"]), tiles["tf"])
        else:
            rw, w1, w3, w2 = lw["moe"]
            x = _moe_ln(x, rw, w1, w3, w2, lw["ln2_g"], lw["ln2_b"], alpha, _tile(t, tiles["ffn"]), tiles["tf"])
        kv.append(pt[:N_KV_FEAT].reshape(N_KV_FEAT // HEAD_DIM, HEAD_DIM, t))
        logf.append(logf_t[:H_FOX])
        conv.append(u_last[SUBLANES - (CONV_K - 1):])
        keep = min(WINDOW, t)
        win.append(pt[T_KW:T_FL, t - keep:].reshape(2 * N_KV_NSA, HEAD_DIM, keep))
    kv = jnp.transpose(jnp.stack(kv), (0, 3, 1, 2))[:, None]
    logf = jnp.transpose(jnp.stack(logf), (0, 2, 1))[:, None]
    conv = jnp.stack(conv)[:, None]
    win = jnp.transpose(jnp.stack(win), (0, 3, 1, 2))[:, None]
    return x, kv, logf, conv, win


TILES = dict(proj=512, flash=512, cmp=256, row=512, ffn=1024, tf=1408, pg=16)


def _prepare_layers(w_in, b_f, conv_w, w_cmp_k, w_cmp_v, w_out, ln1_g, ln1_b, ln2_g, ln2_b,
                    dense_w1, dense_w3, dense_w2, router_w, moe_w1, moe_w3, moe_w2):
    layers = []
    d = w_in.shape[1]
    for l in range(w_in.shape[0]):
        lw = dict(pk=_pack_layer_weights(w_in[l], b_f[l], conv_w[l], w_cmp_k[l], w_cmp_v[l]),
                  w_out=w_out[l].astype(bf16), ln1_g=ln1_g[l], ln1_b=ln1_b[l], ln2_g=ln2_g[l], ln2_b=ln2_b[l],
                  dense=None, moe=None)
        if l % 2 == 0:
            lw["dense"] = (dense_w1[l // 2].astype(bf16), dense_w3[l // 2].astype(bf16), dense_w2[l // 2].astype(bf16))
        else:
            rw = jnp.concatenate([router_w[l // 2], jnp.zeros((d, LANES - N_EXPERTS), f32)], axis=1)
            lw["moe"] = (rw, moe_w1[l // 2].astype(bf16), moe_w3[l // 2].astype(bf16), moe_w2[l // 2].astype(bf16))
        layers.append(lw)
    return layers


R_KV, R_WIN, R_ALL_END = R_END, R_END + N_KV_FEAT, R_END + N_KV_FEAT + 2 * N_KV_NSA * HEAD_DIM
PAGE = 128
ROW_FOX, ROW_CMP, ROW_SLC = 0, 2 * H_FOX, 2 * H_FOX + 2 * N_KV_NSA
FOX_W = H_FOX * HEAD_DIM
Q_ROWS = 16


def _online_update(s, v_t, m_ref, l_ref, acc_ref):
    m_old = m_ref[...]
    m_new = jnp.maximum(m_old, jnp.max(s, axis=1, keepdims=True))
    alpha = jnp.exp(m_old - m_new)
    p = jnp.exp(s - m_new)
    l_ref[...] = alpha * l_ref[...] + jnp.sum(p, axis=1, keepdims=True)
    acc_ref[...] = alpha * acc_ref[...] + _nt_dot(p.astype(bf16), v_t)
    m_ref[...] = m_new


def _decode_kernel(pt_ref, qbd_ref, knew_ref, vnew_ref, cq_ref, ck_ref, wc_ref, *rest, pg):
    kv_refs, cmp_refs, lf_refs = rest[:pg], rest[pg:2 * pg], rest[2 * pg:3 * pg]
    o_ref, hh_ref, m_ref, l_ref, acc_ref, carry_ref, at_ref, kt_s, vt_s, lf_s = rest[3 * pg:]
    j = pl.program_id(1)
    width = pg * PAGE

    @pl.when(j == 0)
    def _():
        m_ref[...] = jnp.full_like(m_ref, NEG)
        l_ref[...] = jnp.zeros_like(l_ref)
        acc_ref[...] = jnp.zeros_like(acc_ref)
        carry_ref[...] = jnp.zeros_like(carry_ref)
        lf_s[...] = jnp.zeros_like(lf_s)

    qbd = (qbd_ref[0] * 0.125).astype(bf16)
    row_head = lax.rem(lax.broadcasted_iota(jnp.int32, (Q_ROWS, 1), 0), H_FOX)
    cq = cq_ref[0]
    for k in range(pg):
        sl = slice(k * PAGE, (k + 1) * PAGE)
        kt_s[:, sl] = kv_refs[k][0, 0, 0:H_FOX].reshape(FOX_W, PAGE).astype(bf16)
        vt_s[:, sl] = kv_refs[k][0, 0, H_FOX:2 * H_FOX].reshape(FOX_W, PAGE).astype(bf16)
        lf_s[0:H_FOX, sl] = lf_refs[k][0, 0]
    lf = lf_s[...]
    lane = lax.broadcasted_iota(jnp.int32, (1, width), 1)
    incl = lf
    shift = 1
    while shift < width:
        incl = incl + jnp.where(lane + shift < width, pltpu.roll(incl, width - shift, 1), 0.0)
        shift *= 2
    suffix = carry_ref[:, 0:1] + (incl - lf)
    carry_ref[...] = carry_ref[...] + incl[:, 0:1]
    bias = jnp.zeros((Q_ROWS, width), f32)
    for h in range(H_FOX):
        bias = jnp.where(row_head == h, suffix[h:h + 1, :], bias)
    s = jnp.dot(qbd, kt_s[...], preferred_element_type=f32) + bias + cq
    _online_update(s, vt_s[...], m_ref, l_ref, acc_ref)
    n_half = pg * (PAGE // CMP_STRIDE)
    for kind in range(2 * N_KV_NSA):
        for k in range(pg):
            at_ref[kind, k * PAGE:(k + 1) * PAGE, :] = cmp_refs[k][0, 0, kind].T
        acc = jnp.zeros((n_half, 2 * HEAD_DIM), f32)
        for r in range(CMP_STRIDE):
            rows = at_ref[kind, pl.ds(r, n_half, stride=CMP_STRIDE), :]
            acc = acc + jnp.dot(rows.astype(bf16), wc_ref[kind, r], preferred_element_type=f32)
        hh_ref[0, kind] = acc

    @pl.when(j == pl.num_programs(1) - 1)
    def _():
        knew = knew_ref[0].astype(bf16)
        s = _nt_dot(qbd, knew)
        trow = lax.shift_right_logical(lax.broadcasted_iota(jnp.int32, (Q_ROWS, 1), 0), 2)
        tcol = lax.broadcasted_iota(jnp.int32, (1, SUBLANES), 1)
        s = jnp.where(tcol <= trow, s + (cq - ck_ref[0]), NEG)
        _online_update(s, vnew_ref[0].T.astype(bf16), m_ref, l_ref, acc_ref)
        o_ref[0] = acc_ref[...] / l_ref[...]


def _decode(l, page_table, qbd, knew, vnew, cq, ck, wc, cache_t, logf_t, pg):
    n, n_pages = page_table.shape
    steps = n_pages // pg

    def page_of(b, j, k, pt):
        return pt[b, (steps - 1 - j) * pg + k]

    per_seq = lambda shp: pl.BlockSpec((1,) + shp, lambda b, j, pt: (b,) + (0,) * len(shp))
    const = lambda shp: pl.BlockSpec(shp, lambda b, j, pt: (0,) * len(shp))
    kv_specs = [pl.BlockSpec((1, 1, 2 * H_FOX, HEAD_DIM, PAGE),
                             lambda b, j, pt, k=k: (l, page_of(b, j, k, pt), ROW_FOX // (2 * H_FOX), 0, 0)) for k in range(pg)]
    cmp_specs = [pl.BlockSpec((1, 1, 2 * N_KV_NSA, HEAD_DIM, PAGE),
                              lambda b, j, pt, k=k: (l, page_of(b, j, k, pt), ROW_CMP // (2 * N_KV_NSA), 0, 0)) for k in range(pg)]
    lf_specs = [pl.BlockSpec((1, 1, H_FOX, PAGE), lambda b, j, pt, k=k: (l, page_of(b, j, k, pt), 0, 0)) for k in range(pg)]
    n_half = n_pages * (PAGE // CMP_STRIDE)
    grid_spec = pltpu.PrefetchScalarGridSpec(
        num_scalar_prefetch=1,
        grid=(n, steps),
        in_specs=[per_seq((Q_ROWS, FOX_W)), per_seq((SUBLANES, FOX_W)), per_seq((SUBLANES, FOX_W)),
                  per_seq((Q_ROWS, 1)), per_seq((Q_ROWS, SUBLANES)),
                  const(wc.shape)] + kv_specs + cmp_specs + lf_specs,
        out_specs=[per_seq((Q_ROWS, FOX_W)),
                   pl.BlockSpec((1, 2 * N_KV_NSA, pg * (PAGE // CMP_STRIDE), 2 * HEAD_DIM),
                                lambda b, j, pt: (b, 0, steps - 1 - j, 0))],
        scratch_shapes=[pltpu.VMEM((Q_ROWS, 1), f32), pltpu.VMEM((Q_ROWS, 1), f32), pltpu.VMEM((Q_ROWS, FOX_W), f32),
                        pltpu.VMEM((SUBLANES, PAGE), f32), pltpu.VMEM((2 * N_KV_NSA, pg * PAGE, HEAD_DIM), f32),
                        pltpu.VMEM((FOX_W, pg * PAGE), bf16), pltpu.VMEM((FOX_W, pg * PAGE), bf16),
                        pltpu.VMEM((SUBLANES, pg * PAGE), f32)])
    return pl.pallas_call(
        functools.partial(_decode_kernel, pg=pg),
        grid_spec=grid_spec,
        out_shape=[jax.ShapeDtypeStruct((n, Q_ROWS, FOX_W), f32),
                   jax.ShapeDtypeStruct((n, 2 * N_KV_NSA, n_half, 2 * HEAD_DIM), f32)],
        compiler_params=_cparams(("parallel", "arbitrary")),
        name="decode_fox_compress",
    )(page_table, qbd, knew, vnew, cq, ck, wc, *([cache_t] * (2 * pg)), *([logf_t] * pg))


def _softmax_two(s_list, v_list, sn, vn):
    m = jnp.max(sn, axis=1, keepdims=True)
    for s in s_list:
        m = jnp.maximum(m, jnp.max(s, axis=1, keepdims=True))
    pn = jnp.exp(sn - m)
    den = jnp.sum(pn, axis=1, keepdims=True)
    out = jnp.zeros((sn.shape[0], vn.shape[1]), f32)
    for c in range(vn.shape[0]):
        out = out + pn[:, c:c + 1] * vn[c:c + 1, :]
    for s, v in zip(s_list, v_list):
        p = jnp.exp(s - m)
        den = den + jnp.sum(p, axis=1, keepdims=True)
        out = out + _nt_dot(p.astype(bf16), v)
    return out / den


def _new_token_scores(qg, kn, slope_col, t):
    tcol = lax.broadcasted_iota(jnp.int32, (1, SUBLANES), 1)
    sn = jnp.zeros((SUBLANES, SUBLANES), f32)
    for c in range(SUBLANES):
        sn = jnp.where(tcol == c, jnp.sum(qg * kn[c:c + 1, :], axis=1, keepdims=True), sn)
    return jnp.where(tcol <= t, sn - slope_col * (t - tcol).astype(f32), NEG)


def _nsa_decode_kernel(idx_ref, pt_ref, qs_ref, ksn_ref, vsn_ref, kwn_ref, vwn_ref, win_ref, *rest, past):
    n_k = N_KV_NSA * TOP_N
    k_refs, v_refs = rest[:n_k], rest[n_k:2 * n_k]
    os_ref, ow_ref = rest[2 * n_k:]
    b, t = pl.program_id(0), pl.program_id(1)
    qpos = past + t
    lane = lax.broadcasted_iota(jnp.int32, (1, PAGE), 1)
    lane_half = lax.shift_right_logical(lane, 6)
    rows = lax.broadcasted_iota(jnp.int32, (SUBLANES, 1), 0)
    n_past_blocks = past // SLC_BLOCK
    wcol = lax.broadcasted_iota(jnp.int32, (1, WINDOW), 1)
    for g in range(N_KV_NSA):
        slope_col = jnp.zeros((SUBLANES, 1), f32)
        for r in range(HEADS_PER_GROUP):
            slope_col = jnp.where(rows == r, _alibi_slope(g * HEADS_PER_GROUP + r), slope_col)
        qg = qs_ref[0, 0, g] * 0.125
        qgb = qg.astype(bf16)
        s_list, v_list = [], []
        for k in range(TOP_N):
            blk = idx_ref[b, t, g * TOP_N + k]
            half = lax.rem(blk, 2)
            kpos = (blk // 2) * PAGE + lane
            ok = jnp.logical_and(lane_half == half, blk < n_past_blocks)
            s = jnp.dot(qgb, k_refs[g * TOP_N + k][0, 0, 0].astype(bf16), preferred_element_type=f32)
            s_list.append(jnp.where(ok, s - slope_col * (qpos - kpos).astype(f32), NEG))
            v_list.append(v_refs[g * TOP_N + k][0, 0, 0].astype(bf16))
        sn = _new_token_scores(qg, ksn_ref[0, g], slope_col, t)
        os_ref[0, 0, g] = _softmax_two(s_list, v_list, sn, vsn_ref[0, g])
        d = WINDOW + t - wcol
        sw = jnp.dot(qgb, win_ref[0, 0, g].astype(bf16), preferred_element_type=f32)
        sw = jnp.where(d <= WINDOW, sw - slope_col * d.astype(f32), NEG)
        swn = _new_token_scores(qg, kwn_ref[0, g], slope_col, t)
        ow_ref[0, 0, g] = _softmax_two([sw], [win_ref[0, 0, N_KV_NSA + g].astype(bf16)], swn, vwn_ref[0, g])


def _nsa_decode(l, idx, page_table, qs, ksn, vsn, kwn, vwn, win_t, cache_t, past):
    n, t_new = idx.shape[0], idx.shape[1]
    n_pages = page_table.shape[1]

    def sel_map(g, k, row0):
        def f(b, t, idx_r, pt_r):
            page = jnp.minimum(idx_r[b, t, g * TOP_N + k] // 2, n_pages - 1)
            return (l, pt_r[b, page], row0 + g, 0, 0)
        return f

    tile = (1, 1, 1, HEAD_DIM, PAGE)
    k_specs = [pl.BlockSpec(tile, sel_map(g, k, ROW_SLC)) for g in range(N_KV_NSA) for k in range(TOP_N)]
    v_specs = [pl.BlockSpec(tile, sel_map(g, k, ROW_SLC + N_KV_NSA)) for g in range(N_KV_NSA) for k in range(TOP_N)]
    new_spec = pl.BlockSpec((1, N_KV_NSA, SUBLANES, HEAD_DIM), lambda b, t, i, p: (b, 0, 0, 0))
    q_spec = pl.BlockSpec((1, 1, N_KV_NSA, SUBLANES, HEAD_DIM), lambda b, t, i, p: (b, t, 0, 0, 0))
    grid_spec = pltpu.PrefetchScalarGridSpec(
        num_scalar_prefetch=2,
        grid=(n, t_new),
        in_specs=[q_spec, new_spec, new_spec, new_spec, new_spec,
                  pl.BlockSpec((1, 1, 2 * N_KV_NSA, HEAD_DIM, WINDOW), lambda b, t, i, p: (l, b, 0, 0, 0))] + k_specs + v_specs,
        out_specs=[q_spec, q_spec])
    n_k = N_KV_NSA * TOP_N
    shp = jax.ShapeDtypeStruct((n, t_new, N_KV_NSA, SUBLANES, HEAD_DIM), f32)
    return pl.pallas_call(
        functools.partial(_nsa_decode_kernel, past=past),
        grid_spec=grid_spec,
        out_shape=[shp, shp],
        compiler_params=_cparams(("parallel", "arbitrary")),
        name="decode_slc_win",
    )(idx, page_table, qs, ksn, vsn, kwn, vwn, win_t, *([cache_t] * (2 * n_k)))


def _conv_sample_kernel(b_ref, c_ref, u_ref, w_ref, o_ref, un_ref):
    u = c_ref[...] * u_ref[...]
    w = w_ref[...]
    y = w[0:1] * pltpu.roll(u, 2, 0) + w[1:2] * pltpu.roll(u, 1, 0) + w[2:3] * u
    o_ref[...] = b_ref[...] * y
    un_ref[...] = u


def _conv_sample(b_ext, c_ext, u_ext, w8):
    shp = jax.ShapeDtypeStruct(b_ext.shape, f32)
    return pl.pallas_call(_conv_sample_kernel, out_shape=[shp, shp], name="conv_sample")(b_ext, c_ext, u_ext, w8)


def _sample_mixer(x, lw, l, cache_t, logf_t, win_t, conv_state, page_table, past, n, t_new, pg):
    pk = lw["pk"]
    m = n * t_new
    assert t_new * H_FOX == Q_ROWS and m == LANES and past % PAGE == 0 and past >= WINDOW
    pr, pt = _project(x, pk["wr_all"], pk["wt"], m)
    lf_t, cum_t = _logf_cumsum(pt, pk["bf8"], m, segment=t_new)
    seq = lambda a: a.reshape(n, t_new, -1)
    pad8 = lambda a: jnp.pad(a, ((0, 0), (0, SUBLANES - t_new), (0, 0)))
    kv_rows = pr[:, R_KV:R_WIN]
    win_rows = pr[:, R_WIN:R_ALL_END]
    head_mask = jnp.asarray((np.arange(FOX_W)[None, :] // HEAD_DIM == np.arange(H_FOX)[:, None]).astype(np.float32))
    qbd = (seq(pr[:, R_QF:R_QF + FOX_W])[:, :, None, :] * head_mask[None, None]).reshape(n, Q_ROWS, FOX_W)
    knew, vnew = pad8(seq(kv_rows[:, 0:FOX_W])), pad8(seq(kv_rows[:, FOX_W:2 * FOX_W]))
    cum = cum_t[:H_FOX].reshape(H_FOX, n, t_new)
    cq = jnp.transpose(cum, (1, 2, 0)).reshape(n, Q_ROWS, 1)
    ck = jnp.broadcast_to(jnp.transpose(cum, (1, 0, 2))[:, None], (n, t_new, H_FOX, t_new)).reshape(n, Q_ROWS, t_new)
    ck = jnp.pad(ck, ((0, 0), (0, 0), (0, SUBLANES - t_new)))
    o16, hh = _decode(l, page_table, qbd, knew, vnew, cq, ck, pk["wc"], cache_t, logf_t, pg)
    o16 = o16.reshape(n, t_new, H_FOX, H_FOX, HEAD_DIM)
    o_fox = jnp.stack([o16[:, :, h, h] for h in range(H_FOX)], axis=2).reshape(m, FOX_W)
    qn = pr[:, R_QN:R_QN + H_NSA * HEAD_DIM]
    q8 = pad8(seq(qn)).reshape(n * SUBLANES, H_NSA * HEAD_DIM)
    o_c8, _, idx8, _ = _cmp_select(q8, 0, hh, past, t_new, SUBLANES)
    o_c = o_c8.reshape(n, SUBLANES, -1)[:, :t_new].reshape(m, -1)
    idx = idx8.reshape(n, SUBLANES, LANES)[:, :t_new, :N_KV_NSA * TOP_N]
    qs = qn.reshape(n, t_new, N_KV_NSA, HEADS_PER_GROUP, HEAD_DIM)
    qs = jnp.pad(qs, ((0, 0), (0, 0), (0, 0), (0, SUBLANES - HEADS_PER_GROUP), (0, 0)))
    grp = lambda a: jnp.pad(jnp.transpose(a.reshape(n, t_new, N_KV_NSA, HEAD_DIM), (0, 2, 1, 3)),
                            ((0, 0), (0, 0), (0, SUBLANES - t_new), (0, 0)))
    gw = N_KV_NSA * HEAD_DIM
    ksn, vsn = grp(kv_rows[:, 6 * gw:7 * gw]), grp(kv_rows[:, 7 * gw:8 * gw])
    kwn, vwn = grp(win_rows[:, 0:gw]), grp(win_rows[:, gw:2 * gw])
    o_s5, o_w5 = _nsa_decode(l, idx, page_table, qs, ksn, vsn, kwn, vwn, win_t, cache_t, past)
    heads = lambda a: a[:, :, :, :HEADS_PER_GROUP].reshape(m, H_NSA * HEAD_DIM)
    o_s, o_w = heads(o_s5), heads(o_w5)
    ext = lambda a, head: jnp.concatenate([head, seq(a)], axis=1).reshape(n * SUBLANES, D_CONV)
    zeros2 = jnp.zeros((n, 2, D_CONV), f32)
    one_head = jnp.concatenate([zeros2, jnp.ones((n, 2, D_CONV), f32)], axis=1)
    b_ext = ext(pr[:, R_BG:R_BG + D_CONV], jnp.zeros((n, 4, D_CONV), f32))
    c_ext = ext(pr[:, R_CG:R_CG + D_CONV], one_head)
    u_ext = ext(pr[:, R_UIN:R_UIN + D_CONV], jnp.concatenate([zeros2, conv_state], axis=1))
    o_conv8, u8 = _conv_sample(b_ext, c_ext, u_ext, pk["w8"])
    o_conv = o_conv8.reshape(n, SUBLANES, D_CONV)[:, SUBLANES - t_new:].reshape(m, D_CONV)
    new_conv = u8.reshape(n, SUBLANES, D_CONV)[:, SUBLANES - (CONV_K - 1):]
    new_kv = kv_rows.reshape(n, t_new, N_KV_FEAT // HEAD_DIM, HEAD_DIM)
    new_logf = jnp.transpose(lf_t[:H_FOX].reshape(H_FOX, n, t_new), (1, 2, 0))
    win_new_t = jnp.transpose(pt[T_KW:T_FL].reshape(2 * N_KV_NSA, HEAD_DIM, n, t_new), (2, 0, 1, 3))
    new_win_t = jnp.concatenate([win_t[l][..., t_new:], win_new_t], axis=-1)
    return pr, o_fox, o_conv, o_c, o_s, o_w, new_kv, new_logf, new_conv, new_win_t


def _trunk_sample(x, layers, alpha, cache_t, logf_t, win_t, state_conv, page_table, tiles):
    n, t_new, d = x.shape
    past = page_table.shape[1] * PAGE
    x = x.reshape(n * t_new, d)
    m = n * t_new
    pg = min(tiles["pg"], page_table.shape[1])
    kv, logf, conv, win = [], [], [], []
    for l, lw in enumerate(layers):
        pr, o_fox, o_conv, o_c, o_s, o_w, nkv, nlogf, nconv, nwin_t = _sample_mixer(
            x, lw, l, cache_t, logf_t, win_t, state_conv[l], page_table, past, n, t_new, pg)
        x = _outproj_ln(x, o_fox, o_conv, o_c, o_s, o_w, pr, R_SM, lw["w_out"], lw["ln1_g"], lw["ln1_b"], alpha, m)
        if lw["moe"] is None:
            w1, w3, w2 = lw["dense"]
            x = _ffn_ln(x, w1, w3, w2, lw["ln2_g"], lw["ln2_b"], alpha, m, tiles["tf"])
        else:
            rw, w1, w3, w2 = lw["moe"]
            x = _moe_ln(x, rw, w1, w3, w2, lw["ln2_g"], lw["ln2_b"], alpha, m, tiles["tf"])
        kv.append(nkv)
        logf.append(nlogf)
        conv.append(nconv)
        win.append(jnp.transpose(nwin_t, (0, 3, 1, 2)))
    return x.reshape(n, t_new, d), jnp.stack(kv), jnp.stack(logf), jnp.stack(conv), jnp.stack(win)


def kernel(x_prompt, x_sample, cache_kv, cache_logf, state_conv, state_win, page_table, w_in, b_f, conv_w, w_cmp_k, w_cmp_v, w_out, ln1_g, ln1_b, ln2_g, ln2_b, dense_w1, dense_w3, dense_w2, router_w, moe_w1, moe_w3, moe_w2):
    depth = w_in.shape[0]
    alpha = (2.0 * depth) ** 0.25
    layers = _prepare_layers(w_in, b_f, conv_w, w_cmp_k, w_cmp_v, w_out, ln1_g, ln1_b, ln2_g, ln2_b,
                             dense_w1, dense_w3, dense_w2, router_w, moe_w1, moe_w3, moe_w2)
    y_p, kv_p, logf_p, conv_p, win_p = _trunk_prompt(x_prompt[0], layers, alpha, TILES)
    cache_t = jnp.transpose(cache_kv, (0, 1, 3, 4, 2))
    logf_t = jnp.transpose(cache_logf, (0, 1, 3, 2))
    win_t = jnp.transpose(state_win, (0, 1, 3, 4, 2))
    y_s, kv_s, logf_s, conv_s, win_s = _trunk_sample(
        x_sample, layers, alpha, cache_t, logf_t, win_t, state_conv, page_table, TILES)
    return (y_p[None], y_s, kv_p, logf_p, conv_p, win_p, kv_s, logf_s, conv_s, win_s)
```

```python
import functools

import numpy as np
import jax
import jax.numpy as jnp
from jax import lax
from jax.experimental import pallas as pl
from jax.experimental.pallas import tpu as pltpu

f32 = jnp.float32
bf16 = jnp.bfloat16

HEAD_DIM = 64
H_FOX = 4
D_CONV = 256
H_NSA = 8
N_KV_NSA = 2
HEADS_PER_GROUP = H_NSA // N_KV_NSA
CONV_K = 3
CMP_STRIDE = 16
SLC_BLOCK = 64
TOP_N = 16
N_LOCAL_BLOCKS = 2
SLC_COVER_W = (1.0, 2.0, 2.0, 2.0, 1.0)
WINDOW = 512
N_EXPERTS = 8
TOP_K = 2
LN_EPS = 1e-5
FORCE_BONUS = 1e4
NEG = -1e30
LANES = 128
SUBLANES = 8
VMEM_LIMIT = 56 * 1024 * 1024
HIGHEST = lax.Precision.HIGHEST

_C_QF, _C_KF, _C_VF, _C_FL = 0, 256, 512, 768
_C_BG, _C_CG, _C_UIN, _C_QN = 772, 1028, 1284, 1540
_C_KC, _C_VC, _C_KS, _C_VS, _C_KW, _C_VW, _C_GN, _C_END = 2052, 2180, 2308, 2436, 2564, 2692, 2820, 2844
R_QN, R_QF, R_BG, R_CG, R_UIN, R_SM, R_END = 0, 512, 768, 1024, 1280, 1536, 1664
SM_GN = 4
T_KF, T_VF, T_KC, T_VC, T_KS, T_VS, T_KW, T_VW, T_FL, T_END = 0, 256, 512, 640, 768, 896, 1024, 1152, 1280, 1288
N_KV_FEAT = 1024


def _cparams(sem):
    return pltpu.CompilerParams(dimension_semantics=sem, vmem_limit_bytes=VMEM_LIMIT)


def _nt_dot(a, b):
    return lax.dot_general(a, b, (((1,), (1,)), ((), ())), preferred_element_type=f32)


def _log_sigmoid(x):
    return -(jnp.maximum(-x, 0.0) + jnp.log1p(jnp.exp(-jnp.abs(x))))


def _sigmoid(x):
    return 1.0 / (1.0 + jnp.exp(-x))


def _proj_kernel(x_ref, wr_ref, wt_ref, pr_ref, pt_ref):
    xb = x_ref[...].astype(bf16)
    pr_ref[...] = jnp.dot(xb, wr_ref[...], preferred_element_type=f32)
    pt_ref[...] = _nt_dot(wt_ref[...], xb)


def _project(x, wr, wt, tm):
    m, d = x.shape
    cr, ct = wr.shape[1], wt.shape[0]
    return pl.pallas_call(
        _proj_kernel,
        grid=(m // tm,),
        in_specs=[pl.BlockSpec((tm, d), lambda i: (i, 0)),
                  pl.BlockSpec((d, cr), lambda i: (0, 0)),
                  pl.BlockSpec((ct, d), lambda i: (0, 0))],
        out_specs=[pl.BlockSpec((tm, cr), lambda i: (i, 0)),
                   pl.BlockSpec((ct, tm), lambda i: (0, i))],
        out_shape=[jax.ShapeDtypeStruct((m, cr), f32), jax.ShapeDtypeStruct((ct, m), f32)],
        compiler_params=_cparams(("parallel",)),
        name="project",
    )(x, wr, wt)


def _logf_cumsum_kernel(f_ref, bf_ref, tri_ref, logf_ref, c_ref, carry_ref):
    @pl.when(pl.program_id(0) == 0)
    def _():
        carry_ref[...] = jnp.zeros_like(carry_ref)

    carry = carry_ref[...]
    n_sub = f_ref.shape[1] // LANES
    for s in range(n_sub):
        sl = slice(s * LANES, (s + 1) * LANES)
        lf = _log_sigmoid(f_ref[:, sl] + bf_ref[...])
        logf_ref[:, sl] = lf
        cs = jnp.dot(lf, tri_ref[...], precision=HIGHEST, preferred_element_type=f32) + carry
        c_ref[:, sl] = cs
        carry = jnp.broadcast_to(cs[:, LANES - 1:LANES], carry.shape)
    carry_ref[...] = carry


def _logf_cumsum(pt, bf8, tc, segment=LANES):
    t = pt.shape[1]
    ii, jj = np.meshgrid(np.arange(LANES), np.arange(LANES), indexing="ij")
    tri = jnp.asarray(((ii <= jj) & (ii // segment == jj // segment)).astype(np.float32))
    return pl.pallas_call(
        _logf_cumsum_kernel,
        grid=(t // tc,),
        in_specs=[pl.BlockSpec((SUBLANES, tc), lambda j: (T_FL // SUBLANES, j)),
                  pl.BlockSpec((SUBLANES, 1), lambda j: (0, 0)),
                  pl.BlockSpec((LANES, LANES), lambda j: (0, 0))],
        out_specs=[pl.BlockSpec((SUBLANES, tc), lambda j: (0, j)),
                   pl.BlockSpec((SUBLANES, tc), lambda j: (0, j))],
        out_shape=[jax.ShapeDtypeStruct((SUBLANES, t), f32)] * 2,
        scratch_shapes=[pltpu.VMEM((SUBLANES, LANES), f32)],
        compiler_params=_cparams(("arbitrary",)),
        name="logf_cumsum",
    )(pt, bf8, tri)


def _alibi_slope(h):
    return float(2.0 ** (-8.0 * (h + 1) / H_NSA))


def _flash_kernel(*refs, mode, n_heads, tq, tk):
    if mode == "fox":
        flags_ref, q_ref, kt_ref, vt_ref, c_ref, cq_ref, o_ref, m_ref, l_ref, acc_ref = refs
    elif mode == "slc":
        flags_ref, q_ref, kt_ref, vt_ref, sel_ref, o_ref, m_ref, l_ref, acc_ref = refs
    else:
        q_ref, kt_ref, vt_ref, o_ref, m_ref, l_ref, acc_ref = refs
    i, j = pl.program_id(0), pl.program_id(1)
    nq, nk = pl.num_programs(0), pl.num_programs(1)

    @pl.when(j == 0)
    def _():
        m_ref[...] = jnp.full_like(m_ref, NEG)
        l_ref[...] = jnp.zeros_like(l_ref)
        acc_ref[...] = jnp.zeros_like(acc_ref)

    if mode == "win":
        kv_tile = jnp.maximum(i - 1 + j, 0)
        active = jnp.logical_or(i > 0, j > 0)
    elif mode == "fox":
        kv_tile = jnp.maximum(i - j, 0)
        active = j <= i
    else:
        kv_tile = j
        active = j <= i
    q0 = i * tq

    def positions():
        qpos = q0 + lax.broadcasted_iota(jnp.int32, (tq, 1), 0)
        kpos = kv_tile * tk + lax.broadcasted_iota(jnp.int32, (1, tk), 1)
        return qpos, kpos

    def attend(h, kv, bias, mask):
        qh = (q_ref[:, h * HEAD_DIM:(h + 1) * HEAD_DIM] * 0.125).astype(bf16)
        kth = kt_ref[kv * HEAD_DIM:(kv + 1) * HEAD_DIM, :].astype(bf16)
        vth = vt_ref[kv * HEAD_DIM:(kv + 1) * HEAD_DIM, :].astype(bf16)
        s = jnp.dot(qh, kth, preferred_element_type=f32) + bias + mask
        m_old = m_ref[h]
        m_new = jnp.maximum(m_old, jnp.max(s, axis=1, keepdims=True))
        alpha = jnp.exp(m_old - m_new)
        p = jnp.exp(s - m_new)
        l_ref[h] = alpha * l_ref[h] + jnp.sum(p, axis=1, keepdims=True)
        acc_ref[h] = alpha * acc_ref[h] + _nt_dot(p.astype(bf16), vth)
        m_ref[h] = m_new

    if mode == "fox":
        @pl.when(jnp.logical_and(active, flags_ref[i * nk + kv_tile] > 0))
        def _():
            qpos, kpos = positions()
            neg = jnp.where(kpos > qpos, NEG, 0.0)
            for h in range(n_heads):
                attend(h, h, cq_ref[h:h + 1, 0:1] - c_ref[h:h + 1, :], neg)
    elif mode == "slc":
        nbs = sel_ref.shape[1] // N_KV_NSA
        for g in range(N_KV_NSA):
            @pl.when(jnp.logical_and(active, flags_ref[(g * nq + i) * nk + kv_tile] > 0))
            def _(g=g):
                qpos, kpos = positions()
                rel = (kpos - q0).astype(f32)
                blk_of_key = lax.shift_right_logical(kpos, 6)
                expand = (lax.broadcasted_iota(jnp.int32, (nbs, 1), 0) == blk_of_key).astype(bf16)
                selg = sel_ref[:, g * nbs:(g + 1) * nbs].astype(bf16)
                hit = jnp.dot(selg, expand, preferred_element_type=f32)
                mask = jnp.where(jnp.logical_and(hit > 0.5, kpos <= qpos), 0.0, NEG)
                for r in range(HEADS_PER_GROUP):
                    h = g * HEADS_PER_GROUP + r
                    attend(h, g, _alibi_slope(h) * rel, mask)
    else:
        @pl.when(active)
        def _():
            qpos, kpos = positions()
            d = qpos - kpos
            neg = jnp.where(jnp.logical_or(d < 0, d > WINDOW), NEG, 0.0)
            rel = (kpos - q0).astype(f32)
            for h in range(n_heads):
                attend(h, h // HEADS_PER_GROUP, _alibi_slope(h) * rel, neg)

    @pl.when(j == nk - 1)
    def _():
        for h in range(n_heads):
            o_ref[:, h * HEAD_DIM:(h + 1) * HEAD_DIM] = acc_ref[h] / l_ref[h]


EXP_DEAD_MARGIN = 110.0


def _fox_tile_flags(pr, pt, c_t, tq):
    t = pr.shape[0]
    nq = t // tq
    q = pr[:, R_QF:R_QF + H_FOX * HEAD_DIM].reshape(nq, tq, H_FOX, HEAD_DIM)
    qn = jnp.sqrt(jnp.max(jnp.sum(q * q, axis=-1), axis=1))
    k = pt[T_KF:T_KF + H_FOX * HEAD_DIM].reshape(H_FOX, HEAD_DIM, nq, tq)
    kn = jnp.sqrt(jnp.max(jnp.sum(k * k, axis=1), axis=-1)).T
    c4 = c_t[:H_FOX].reshape(H_FOX, nq, tq)
    cq0, cend = c4[:, :, 0].T, c4[:, :, -1].T
    rounding = 1.01
    ub = rounding * 0.125 * qn[:, None, :] * kn[None, :, :] + (cq0[:, None, :] - cend[None, :, :])
    self_lb = -rounding * 0.125 * qn * kn
    dead = jnp.all(ub - self_lb[:, None, :] < -EXP_DEAD_MARGIN, axis=-1)
    live = jnp.logical_or(jnp.logical_not(dead), jnp.eye(nq, dtype=bool))
    return live.reshape(-1).astype(jnp.int32)


def _slc_tile_flags(anyb, tq, tq_cmp, t):
    nq = t // tq
    nbs = anyb.shape[1] // N_KV_NSA
    per_kv = tq // SLC_BLOCK
    a = anyb[::SUBLANES].reshape(nq, tq // tq_cmp, N_KV_NSA, nbs // per_kv, per_kv)
    flags = (jnp.max(a, axis=(1, 4)) > 0.5)[:, :, :nq]
    return jnp.transpose(flags, (1, 0, 2)).reshape(-1).astype(jnp.int32)


def _flash(mode, pr, pt, q_col, kt_row, vt_row, tq, extra=()):
    t = pr.shape[0]
    n_heads = H_FOX if mode == "fox" else H_NSA
    n_kv = H_FOX if mode == "fox" else N_KV_NSA
    qw, kw = n_heads * HEAD_DIM, n_kv * HEAD_DIM
    tk = tq
    nq = t // tq
    nk = 2 if mode == "win" else nq
    if mode == "win":
        kv_map = lambda i, j, *_: jnp.maximum(i - 1 + j, 0)
    elif mode == "fox":
        def kv_map(i, j, flags):
            kv = jnp.maximum(i - j, 0)
            return jnp.where(flags[i * nk + kv] > 0, kv, i)
    elif mode == "slc":
        def kv_map(i, j, flags):
            kv = jnp.minimum(j, i)
            used = jnp.logical_or(flags[i * nk + kv] > 0, flags[(nq + i) * nk + kv] > 0)
            return jnp.where(used, kv, 0)
    in_specs = [pl.BlockSpec((tq, qw), lambda i, j, *_: (i, q_col // qw)),
                pl.BlockSpec((kw, tk), lambda i, j, *p: (kt_row // kw, kv_map(i, j, *p))),
                pl.BlockSpec((kw, tk), lambda i, j, *p: (vt_row // kw, kv_map(i, j, *p)))]
    prefetch = ()
    if mode == "fox":
        (ct,) = extra
        in_specs += [pl.BlockSpec((SUBLANES, tk), lambda i, j, *p: (0, kv_map(i, j, *p))),
                     pl.BlockSpec((SUBLANES, LANES), lambda i, j, *_: (0, i * (tq // LANES)))]
        args = (pr, pt, pt, ct, ct)
        prefetch = (_fox_tile_flags(pr, pt, ct, tq),)
    elif mode == "slc":
        sel, flags = extra
        in_specs += [pl.BlockSpec((tq, sel.shape[1]), lambda i, j, *_: (i, 0))]
        args = (pr, pt, pt, sel)
        prefetch = (flags,)
    else:
        args = (pr, pt, pt)
    grid_spec = pltpu.PrefetchScalarGridSpec(
        num_scalar_prefetch=len(prefetch),
        grid=(nq, nk),
        in_specs=in_specs,
        out_specs=pl.BlockSpec((tq, qw), lambda i, j, *_: (i, 0)),
        scratch_shapes=[pltpu.VMEM((n_heads, tq, 1), f32), pltpu.VMEM((n_heads, tq, 1), f32),
                        pltpu.VMEM((n_heads, tq, HEAD_DIM), f32)])
    return pl.pallas_call(
        functools.partial(_flash_kernel, mode=mode, n_heads=n_heads, tq=tq, tk=tk),
        grid_spec=grid_spec,
        out_shape=jax.ShapeDtypeStruct((t, qw), f32),
        compiler_params=_cparams(("parallel", "arbitrary")),
        name="flash_" + mode,
    )(*prefetch, *args)


def _compress_tile(a, at_ref, wc):
    tb = a.shape[1]
    at_ref[...] = a.T
    n_half = tb // CMP_STRIDE
    acc = jnp.zeros((n_half, 2 * HEAD_DIM), f32)
    for r in range(CMP_STRIDE):
        rows = at_ref[pl.ds(r, n_half, stride=CMP_STRIDE), :]
        acc = acc + jnp.dot(rows.astype(bf16), wc(r), preferred_element_type=f32)
    return acc


def _compress_kernel(a_ref, wc_ref, o_ref, at_ref):
    o_ref[0] = _compress_tile(a_ref[...], at_ref, lambda r: wc_ref[0, r])


def _compress(pt, wc, tb):
    t = pt.shape[1]
    base = T_KC // HEAD_DIM
    return pl.pallas_call(
        _compress_kernel,
        grid=(2 * N_KV_NSA, t // tb),
        in_specs=[pl.BlockSpec((HEAD_DIM, tb), lambda k, j: (base + k, j)),
                  pl.BlockSpec((1, CMP_STRIDE, HEAD_DIM, 2 * HEAD_DIM), lambda k, j: (k, 0, 0, 0))],
        out_specs=pl.BlockSpec((1, tb // CMP_STRIDE, 2 * HEAD_DIM), lambda k, j: (k, j, 0)),
        out_shape=jax.ShapeDtypeStruct((2 * N_KV_NSA, t // CMP_STRIDE, 2 * HEAD_DIM), f32),
        scratch_shapes=[pltpu.VMEM((tb, HEAD_DIM), f32)],
        compiler_params=_cparams(("parallel", "parallel")),
        name="compress",
    )(pt, wc)


def _cmp_select_kernel(q_ref, hh_ref, cov_ref, oc_ref, sel_ref, idx_ref, any_ref, *, past, tq, nbs):
    n_half = hh_ref.shape[2]
    nbsp = cov_ref.shape[1]
    t0 = pl.program_id(1) * tq
    qpos = past + t0 + lax.broadcasted_iota(jnp.int32, (tq, 1), 0)
    cmp_end = lax.broadcasted_iota(jnp.int32, (1, n_half), 1) * CMP_STRIDE + (2 * CMP_STRIDE - 1)
    valid_c = cmp_end <= qpos
    dist = (qpos - cmp_end).astype(f32)
    blk = lax.broadcasted_iota(jnp.int32, (1, nbsp), 1)
    blk_f = blk.astype(f32)
    cur = lax.shift_right_logical(qpos, 6)
    valid_s = blk <= cur
    forced = jnp.logical_or(blk == 0, blk >= cur - (N_LOCAL_BLOCKS - 1))
    lane = lax.broadcasted_iota(jnp.int32, (1, LANES), 1)
    idx_acc = jnp.zeros((tq, LANES), f32)
    scores = []
    for g in range(N_KV_NSA):
        hk = hh_ref[0, g]
        hv = hh_ref[0, N_KV_NSA + g]
        ck = hk[:, :HEAD_DIM] + pltpu.roll(hk, n_half - 1, 0)[:, HEAD_DIM:]
        cv = hv[:, :HEAD_DIM] + pltpu.roll(hv, n_half - 1, 0)[:, HEAD_DIM:]
        ckb, cvb = ck.astype(bf16), cv.astype(bf16)
        imp = jnp.zeros((tq, n_half), f32)
        for r in range(HEADS_PER_GROUP):
            h = g * HEADS_PER_GROUP + r
            qh = (q_ref[:, h * HEAD_DIM:(h + 1) * HEAD_DIM] * 0.125).astype(bf16)
            s = _nt_dot(qh, ckb) - _alibi_slope(h) * dist
            s = jnp.where(valid_c, s, -jnp.inf)
            m = jnp.max(s, axis=1, keepdims=True)
            m = jnp.where(m == -jnp.inf, 0.0, m)
            e = jnp.exp(s - m)
            p = e / jnp.maximum(jnp.sum(e, axis=1, keepdims=True), 1e-30)
            oc_ref[:, h * HEAD_DIM:(h + 1) * HEAD_DIM] = jnp.dot(p.astype(bf16), cvb, preferred_element_type=f32)
            imp = imp + p
        score = jnp.zeros((tq, nbsp), f32)
        rest = imp
        for _ in range(3):
            piece = rest.astype(bf16)
            score = score + jnp.dot(piece, cov_ref[...], preferred_element_type=f32)
            rest = rest - piece.astype(f32)
        score = jnp.where(valid_s, score + jnp.where(forced, FORCE_BONUS, 0.0), -FORCE_BONUS)
        scores.append(jnp.where(blk < nbs, score, -jnp.inf))
    selm = [jnp.zeros((tq, nbsp), f32) for _ in range(N_KV_NSA)]
    for k in range(min(TOP_N, nbs)):
        for g in range(N_KV_NSA):
            mx = jnp.max(scores[g], axis=1, keepdims=True)
            ix = jnp.min(jnp.where(scores[g] == mx, blk_f, 1e9), axis=1, keepdims=True)
            hit = blk_f == ix
            selm[g] = jnp.where(hit, 1.0, selm[g])
            scores[g] = jnp.where(hit, -jnp.inf, scores[g])
            idx_acc = jnp.where(lane == g * TOP_N + k, ix, idx_acc)
    for g in range(N_KV_NSA):
        sel_ref[:, g * nbsp:(g + 1) * nbsp] = selm[g]
        any_ref[:, g * nbsp:(g + 1) * nbsp] = jnp.broadcast_to(jnp.max(selm[g], axis=0, keepdims=True), (SUBLANES, nbsp))
    idx_ref[...] = idx_acc.astype(jnp.int32)


def _cover_matrix(n_half, nbs, nbsp):
    cov = np.zeros((n_half, nbsp), np.float32)
    ratio = SLC_BLOCK // CMP_STRIDE
    for jb in range(nbs):
        for mm, w in enumerate(SLC_COVER_W):
            c = ratio * jb + mm - 1
            if 0 <= c < n_half - 1:
                cov[c, jb] += w
    return jnp.asarray(cov).astype(bf16)


def _cmp_select(q, q_col, hh, past, t_len, tq):
    hh = hh.reshape(-1, 2 * N_KV_NSA, hh.shape[-2], hh.shape[-1])
    n = hh.shape[0]
    n_half = hh.shape[2]
    tp = q.shape[0] // n
    nbs = -(-(past + t_len) // SLC_BLOCK)
    nbsp = -(-nbs // LANES) * LANES
    cov = _cover_matrix(n_half, nbs, nbsp)
    qw = H_NSA * HEAD_DIM
    hh4 = hh
    nt = tp // tq
    return pl.pallas_call(
        functools.partial(_cmp_select_kernel, past=past, tq=tq, nbs=nbs),
        grid=(n, nt),
        in_specs=[pl.BlockSpec((tq, qw), lambda b, i: (b * nt + i, q_col // qw)),
                  pl.BlockSpec((1, 2 * N_KV_NSA, n_half, 2 * HEAD_DIM), lambda b, i: (b, 0, 0, 0)),
                  pl.BlockSpec((n_half, nbsp), lambda b, i: (0, 0))],
        out_specs=[pl.BlockSpec((tq, qw), lambda b, i: (b * nt + i, 0)),
                   pl.BlockSpec((tq, N_KV_NSA * nbsp), lambda b, i: (b * nt + i, 0)),
                   pl.BlockSpec((tq, LANES), lambda b, i: (b * nt + i, 0)),
                   pl.BlockSpec((SUBLANES, N_KV_NSA * nbsp), lambda b, i: (b * nt + i, 0))],
        out_shape=[jax.ShapeDtypeStruct((n * tp, qw), f32),
                   jax.ShapeDtypeStruct((n * tp, N_KV_NSA * nbsp), f32),
                   jax.ShapeDtypeStruct((n * tp, LANES), jnp.int32),
                   jax.ShapeDtypeStruct((n * nt * SUBLANES, N_KV_NSA * nbsp), f32)],
        compiler_params=_cparams(("parallel", "parallel")),
        name="cmp_select",
    )(q, hh4, cov)


def _conv_kernel(b_ref, c_ref, u_ref, ch_ref, uh_ref, st_ref, w_ref, o_ref, last_ref):
    i = pl.program_id(0)
    u = c_ref[...] * u_ref[...]
    halo = jnp.where(i == 0, st_ref[...], ch_ref[...] * uh_ref[...])
    ue = jnp.concatenate([halo, u], axis=0)
    u1 = pltpu.roll(ue, 1, 0)[SUBLANES:]
    u2 = pltpu.roll(ue, 2, 0)[SUBLANES:]
    w = w_ref[...]
    y = w[0:1] * u2 + w[1:2] * u1 + w[2:3] * u
    o_ref[...] = b_ref[...] * y
    last_ref[...] = u[u.shape[0] - SUBLANES:]


def _conv_prompt(pr, state8, w8, tm):
    t = pr.shape[0]
    cb = lambda off: off // D_CONV
    hb = tm // SUBLANES
    return pl.pallas_call(
        _conv_kernel,
        grid=(t // tm,),
        in_specs=[pl.BlockSpec((tm, D_CONV), lambda i: (i, cb(R_BG))),
                  pl.BlockSpec((tm, D_CONV), lambda i: (i, cb(R_CG))),
                  pl.BlockSpec((tm, D_CONV), lambda i: (i, cb(R_UIN))),
                  pl.BlockSpec((SUBLANES, D_CONV), lambda i: (jnp.maximum(i * hb - 1, 0), cb(R_CG))),
                  pl.BlockSpec((SUBLANES, D_CONV), lambda i: (jnp.maximum(i * hb - 1, 0), cb(R_UIN))),
                  pl.BlockSpec((SUBLANES, D_CONV), lambda i: (0, 0)),
                  pl.BlockSpec((SUBLANES, D_CONV), lambda i: (0, 0))],
        out_specs=[pl.BlockSpec((tm, D_CONV), lambda i: (i, 0)),
                   pl.BlockSpec((SUBLANES, D_CONV), lambda i: (0, 0))],
        out_shape=[jax.ShapeDtypeStruct((t, D_CONV), f32), jax.ShapeDtypeStruct((SUBLANES, D_CONV), f32)],
        compiler_params=_cparams(("arbitrary",)),
        name="conv",
    )(pr, pr, pr, pr, pr, state8, w8)


def _layernorm(z, g, b):
    mu = jnp.mean(z, axis=-1, keepdims=True)
    zc = z - mu
    var = jnp.mean(zc * zc, axis=-1, keepdims=True)
    return zc * lax.rsqrt(var + LN_EPS) * g + b


def _outproj_kernel(x_ref, fox_ref, conv_ref, oc_ref, os_ref, ow_ref, sm_ref, ex_ref, w_ref, g_ref, b_ref,
                    o_ref, *, alpha):
    sig = _sigmoid(sm_ref[...])
    nsa = None
    for br, br_ref in enumerate((oc_ref, os_ref, ow_ref)):
        gate = jnp.dot(sig, ex_ref[br], precision=HIGHEST, preferred_element_type=f32)
        term = gate * br_ref[...]
        nsa = term if nsa is None else nsa + term
    a0, a1 = H_FOX * HEAD_DIM, H_FOX * HEAD_DIM + D_CONV
    mix = jnp.dot(fox_ref[...].astype(bf16), w_ref[0:a0, :], preferred_element_type=f32)
    mix = mix + jnp.dot(conv_ref[...].astype(bf16), w_ref[a0:a1, :], preferred_element_type=f32)
    mix = mix + jnp.dot(nsa.astype(bf16), w_ref[a1:, :], preferred_element_type=f32)
    o_ref[...] = _layernorm(alpha * x_ref[...] + mix, g_ref[...], b_ref[...])


def _gate_expand():
    ex = np.zeros((3, LANES, H_NSA * HEAD_DIM), np.float32)
    for h in range(H_NSA):
        for br in range(3):
            ex[br, SM_GN + h * 3 + br, h * HEAD_DIM:(h + 1) * HEAD_DIM] = 1.0
    return jnp.asarray(ex)


def _outproj_ln(x, o_fox, o_conv, o_c, o_s, o_w, small, sm_col, w_out, g, b, alpha, tm):
    m, d = x.shape
    row = lambda w: pl.BlockSpec((tm, w), lambda i: (i, 0))
    full = lambda shp: pl.BlockSpec(shp, lambda i: (0,) * len(shp))
    nw = H_NSA * HEAD_DIM
    return pl.pallas_call(
        functools.partial(_outproj_kernel, alpha=alpha),
        grid=(m // tm,),
        in_specs=[row(d), row(H_FOX * HEAD_DIM), row(D_CONV), row(nw), row(nw), row(nw),
                  pl.BlockSpec((tm, LANES), lambda i: (i, sm_col // LANES)),
                  full((3, LANES, nw)), full(w_out.shape), full((1, d)), full((1, d))],
        out_specs=row(d),
        out_shape=jax.ShapeDtypeStruct((m, d), f32),
        compiler_params=_cparams(("parallel",)),
        name="outproj_ln",
    )(x, o_fox, o_conv, o_c, o_s, o_w, small, _gate_expand(), w_out, g.reshape(1, d), b.reshape(1, d))


def _swiglu_partial(xb, w1, w3, w2):
    h1 = jnp.dot(xb, w1, preferred_element_type=f32)
    h3 = jnp.dot(xb, w3, preferred_element_type=f32)
    hh = (h1 * _sigmoid(h1) * h3).astype(bf16)
    return jnp.dot(hh, w2, preferred_element_type=f32)


def _ffn_kernel(x_ref, w1_ref, w3_ref, w2_ref, g_ref, b_ref, o_ref, acc_ref, *, alpha):
    f = pl.program_id(1)

    @pl.when(f == 0)
    def _():
        acc_ref[...] = jnp.zeros_like(acc_ref)

    acc_ref[...] += _swiglu_partial(x_ref[...].astype(bf16), w1_ref[...], w3_ref[...], w2_ref[...])

    @pl.when(f == pl.num_programs(1) - 1)
    def _():
        o_ref[...] = _layernorm(alpha * x_ref[...] + acc_ref[...], g_ref[...], b_ref[...])


def _ffn_ln(x, w1, w3, w2, g, b, alpha, tm, tf):
    m, d = x.shape
    ff = w1.shape[1]
    return pl.pallas_call(
        functools.partial(_ffn_kernel, alpha=alpha),
        grid=(m // tm, ff // tf),
        in_specs=[pl.BlockSpec((tm, d), lambda i, f: (i, 0)),
                  pl.BlockSpec((d, tf), lambda i, f: (0, f)),
                  pl.BlockSpec((d, tf), lambda i, f: (0, f)),
                  pl.BlockSpec((tf, d), lambda i, f: (f, 0)),
                  pl.BlockSpec((1, d), lambda i, f: (0, 0)),
                  pl.BlockSpec((1, d), lambda i, f: (0, 0))],
        out_specs=pl.BlockSpec((tm, d), lambda i, f: (i, 0)),
        out_shape=jax.ShapeDtypeStruct((m, d), f32),
        scratch_shapes=[pltpu.VMEM((tm, d), f32)],
        compiler_params=_cparams(("parallel", "arbitrary")),
        name="ffn_ln",
    )(x, w1, w3, w2, g.reshape(1, d), b.reshape(1, d))


def _moe_kernel(x_ref, r_ref, w1_ref, w3_ref, w2_ref, g_ref, b_ref, o_ref, acc_ref, gate_ref, *, alpha):
    e, f = pl.program_id(1), pl.program_id(2)
    lane = lax.broadcasted_iota(jnp.int32, (1, LANES), 1)

    @pl.when(jnp.logical_and(e == 0, f == 0))
    def _():
        acc_ref[...] = jnp.zeros_like(acc_ref)
        logits = jnp.dot(x_ref[...], r_ref[...], precision=HIGHEST, preferred_element_type=f32)
        logits = jnp.where(lane < N_EXPERTS, logits, -jnp.inf)
        mx = jnp.max(logits, axis=1, keepdims=True)
        ex = jnp.exp(logits - mx)
        probs = ex / jnp.sum(ex, axis=1, keepdims=True)
        lane_f = lane.astype(f32)
        work = jnp.where(lane < N_EXPERTS, probs, -1.0)
        picked = jnp.zeros_like(probs)
        for _ in range(TOP_K):
            top = jnp.max(work, axis=1, keepdims=True)
            ix = jnp.min(jnp.where(work == top, lane_f, 1e9), axis=1, keepdims=True)
            hit = lane_f == ix
            picked = jnp.where(hit, probs, picked)
            work = jnp.where(hit, -1.0, work)
        gate_ref[...] = picked / jnp.sum(picked, axis=1, keepdims=True)

    part = _swiglu_partial(x_ref[...].astype(bf16), w1_ref[0], w3_ref[0], w2_ref[0])
    gate_e = jnp.sum(jnp.where(lane == e, gate_ref[...], 0.0), axis=1, keepdims=True)
    acc_ref[...] += gate_e * part

    @pl.when(jnp.logical_and(e == pl.num_programs(1) - 1, f == pl.num_programs(2) - 1))
    def _():
        o_ref[...] = _layernorm(alpha * x_ref[...] + acc_ref[...], g_ref[...], b_ref[...])


def _moe_ln(x, router, w1, w3, w2, g, b, alpha, tm, tf):
    m, d = x.shape
    ne, _, ff = w1.shape
    return pl.pallas_call(
        functools.partial(_moe_kernel, alpha=alpha),
        grid=(m // tm, ne, ff // tf),
        in_specs=[pl.BlockSpec((tm, d), lambda i, e, f: (i, 0)),
                  pl.BlockSpec((d, LANES), lambda i, e, f: (0, 0)),
                  pl.BlockSpec((1, d, tf), lambda i, e, f: (e, 0, f)),
                  pl.BlockSpec((1, d, tf), lambda i, e, f: (e, 0, f)),
                  pl.BlockSpec((1, tf, d), lambda i, e, f: (e, f, 0)),
                  pl.BlockSpec((1, d), lambda i, e, f: (0, 0)),
                  pl.BlockSpec((1, d), lambda i, e, f: (0, 0))],
        out_specs=pl.BlockSpec((tm, d), lambda i, e, f: (i, 0)),
        out_shape=jax.ShapeDtypeStruct((m, d), f32),
        scratch_shapes=[pltpu.VMEM((tm, d), f32), pltpu.VMEM((tm, LANES), f32)],
        compiler_params=_cparams(("parallel", "arbitrary", "arbitrary")),
        name="moe_ln",
    )(x, router, w1, w3, w2, g.reshape(1, d), b.reshape(1, d))


def _pack_layer_weights(w_in, b_f, conv_w, w_cmp_k, w_cmp_v):
    d = w_in.shape[0]
    cols = lambda a, b: w_in[:, a:b]
    small = jnp.concatenate([cols(_C_FL, _C_BG), cols(_C_GN, _C_END),
                             jnp.zeros((d, LANES - H_FOX - 3 * H_NSA), w_in.dtype)], axis=1)
    wr = jnp.concatenate([cols(_C_QN, _C_KC), cols(_C_QF, _C_KF), cols(_C_BG, _C_QN), small], axis=1)
    kv_cols = jnp.concatenate([cols(_C_KF, _C_FL), cols(_C_KC, _C_KW)], axis=1)
    win_cols = cols(_C_KW, _C_GN)
    fl = jnp.concatenate([cols(_C_FL, _C_BG), jnp.zeros((d, SUBLANES - H_FOX), w_in.dtype)], axis=1)
    wt = jnp.concatenate([kv_cols, win_cols, fl], axis=1).T
    wr_all = jnp.concatenate([wr, kv_cols, win_cols], axis=1)
    bf8 = jnp.concatenate([b_f, jnp.zeros((SUBLANES - H_FOX,), b_f.dtype)]).reshape(SUBLANES, 1)
    w8 = jnp.concatenate([conv_w, jnp.zeros((SUBLANES - CONV_K, D_CONV), conv_w.dtype)], axis=0)

    def wc_of(w):
        lo = w[:CMP_STRIDE * HEAD_DIM].reshape(CMP_STRIDE, HEAD_DIM, HEAD_DIM)
        hi = w[CMP_STRIDE * HEAD_DIM:].reshape(CMP_STRIDE, HEAD_DIM, HEAD_DIM)
        return jnp.concatenate([lo, hi], axis=-1)

    wck, wcv = wc_of(w_cmp_k), wc_of(w_cmp_v)
    wc = jnp.stack([wck] * N_KV_NSA + [wcv] * N_KV_NSA).astype(bf16)
    return dict(wr=wr.astype(bf16), wr_all=wr_all.astype(bf16), wt=wt.astype(bf16), bf8=bf8, w8=w8, wc=wc)


def _tile(t, want):
    return min(t, want)


def _prompt_mixer(x, pk, w_out_b, tiles):
    t = x.shape[0]
    pr, pt = _project(x, pk["wr"], pk["wt"], _tile(t, tiles["proj"]))
    logf_t, c_t = _logf_cumsum(pt, pk["bf8"], _tile(t, 2048))
    tq = _tile(t, tiles["flash"])
    o_fox = _flash("fox", pr, pt, R_QF, T_KF, T_VF, tq, (c_t,))
    hh = _compress(pt, pk["wc"], _tile(t, 2048))
    tq_cmp = _tile(t, tiles["cmp"])
    o_c, sel, _, anyb = _cmp_select(pr, R_QN, hh, 0, t, tq_cmp)
    o_s = _flash("slc", pr, pt, R_QN, T_KS, T_VS, tq, (sel, _slc_tile_flags(anyb, tq, tq_cmp, t)))
    o_w = _flash("win", pr, pt, R_QN, T_KW, T_VW, _tile(t, WINDOW))
    state8 = jnp.zeros((SUBLANES, D_CONV), f32)
    o_conv, u_last = _conv_prompt(pr, state8, pk["w8"], _tile(t, tiles["proj"]))
    return pr, pt, logf_t, o_fox, o_conv, o_c, o_s, o_w, u_last


def _trunk_prompt(x, layers, alpha, tiles):
    t = x.shape[0]
    kv, logf, conv, win = [], [], [], []
    for lw in layers:
        pr, pt, logf_t, o_fox, o_conv, o_c, o_s, o_w, u_last = _prompt_mixer(x, lw["pk"], lw["w_out"], tiles)
        tm = _tile(t, tiles["row"])
        x = _outproj_ln(x, o_fox, o_conv, o_c, o_s, o_w, pr, R_SM, lw["w_out"], lw["ln1_g"], lw["ln1_b"], alpha, tm)
        if lw["moe"] is None:
            w1, w3, w2 = lw["dense"]
            x = _ffn_ln(x, w1, w3, w2, lw["ln2_g"], lw["ln2_b"], alpha, _tile(t, tiles["ffn"]), tiles["tf"])
        else:
            rw, w1, w3, w2 = lw["moe"]
            x = _moe_ln(x, rw, w1, w3, w2, lw["ln2_g"], lw["ln2_b"], alpha, _tile(t, tiles["ffn"]), tiles["tf"])
        kv.append(pt[:N_KV_FEAT].reshape(N_KV_FEAT // HEAD_DIM, HEAD_DIM, t))
        logf.append(logf_t[:H_FOX])
        conv.append(u_last[SUBLANES - (CONV_K - 1):])
        keep = min(WINDOW, t)
        win.append(pt[T_KW:T_FL, t - keep:].reshape(2 * N_KV_NSA, HEAD_DIM, keep))
    kv = jnp.transpose(jnp.stack(kv), (0, 3, 1, 2))[:, None]
    logf = jnp.transpose(jnp.stack(logf), (0, 2, 1))[:, None]
    conv = jnp.stack(conv)[:, None]
    win = jnp.transpose(jnp.stack(win), (0, 3, 1, 2))[:, None]
    return x, kv, logf, conv, win


TILES = dict(proj=512, flash=512, cmp=256, row=512, ffn=1024, tf=1408, pg=16)


def _prepare_layers(w_in, b_f, conv_w, w_cmp_k, w_cmp_v, w_out, ln1_g, ln1_b, ln2_g, ln2_b,
                    dense_w1, dense_w3, dense_w2, router_w, moe_w1, moe_w3, moe_w2):
    layers = []
    d = w_in.shape[1]
    for l in range(w_in.shape[0]):
        lw = dict(pk=_pack_layer_weights(w_in[l], b_f[l], conv_w[l], w_cmp_k[l], w_cmp_v[l]),
                  w_out=w_out[l].astype(bf16), ln1_g=ln1_g[l], ln1_b=ln1_b[l], ln2_g=ln2_g[l], ln2_b=ln2_b[l],
                  dense=None, moe=None)
        if l % 2 == 0:
            lw["dense"] = (dense_w1[l // 2].astype(bf16), dense_w3[l // 2].astype(bf16), dense_w2[l // 2].astype(bf16))
        else:
            rw = jnp.concatenate([router_w[l // 2], jnp.zeros((d, LANES - N_EXPERTS), f32)], axis=1)
            lw["moe"] = (rw, moe_w1[l // 2].astype(bf16), moe_w3[l // 2].astype(bf16), moe_w2[l // 2].astype(bf16))
        layers.append(lw)
    return layers


R_KV, R_WIN, R_ALL_END = R_END, R_END + N_KV_FEAT, R_END + N_KV_FEAT + 2 * N_KV_NSA * HEAD_DIM
PAGE = 128
ROW_FOX, ROW_CMP, ROW_SLC = 0, 2 * H_FOX, 2 * H_FOX + 2 * N_KV_NSA
FOX_W = H_FOX * HEAD_DIM
Q_ROWS = 16


def _online_update(s, v_t, m_ref, l_ref, acc_ref):
    m_old = m_ref[...]
    m_new = jnp.maximum(m_old, jnp.max(s, axis=1, keepdims=True))
    alpha = jnp.exp(m_old - m_new)
    p = jnp.exp(s - m_new)
    l_ref[...] = alpha * l_ref[...] + jnp.sum(p, axis=1, keepdims=True)
    acc_ref[...] = alpha * acc_ref[...] + _nt_dot(p.astype(bf16), v_t)
    m_ref[...] = m_new


def _decode_kernel(pt_ref, qbd_ref, knew_ref, vnew_ref, cq_ref, ck_ref, wc_ref, *rest, pg):
    page_refs, lf_refs = rest[:pg], rest[pg:2 * pg]
    perm_ref, o_ref, hh_ref, m_ref, l_ref, acc_ref, carry_ref, at_ref, kt_s, vt_s, lf_s = rest[2 * pg:]
    j = pl.program_id(1)
    width = pg * PAGE

    @pl.when(j == 0)
    def _():
        m_ref[...] = jnp.full_like(m_ref, NEG)
        l_ref[...] = jnp.zeros_like(l_ref)
        acc_ref[...] = jnp.zeros_like(acc_ref)
        carry_ref[...] = jnp.zeros_like(carry_ref)
        lf_s[...] = jnp.zeros_like(lf_s)

    qbd = (qbd_ref[0] * 0.125).astype(bf16)
    row_head = lax.rem(lax.broadcasted_iota(jnp.int32, (Q_ROWS, 1), 0), H_FOX)
    cq = cq_ref[0]
    for k in range(pg):
        sl = slice(k * PAGE, (k + 1) * PAGE)
        kt_s[:, sl] = page_refs[k][0, 0, 0:H_FOX].reshape(FOX_W, PAGE).astype(bf16)
        vt_s[:, sl] = page_refs[k][0, 0, H_FOX:2 * H_FOX].reshape(FOX_W, PAGE).astype(bf16)
        lf_s[0:H_FOX, sl] = lf_refs[k][0, 0]
    lf = lf_s[...]
    lane = lax.broadcasted_iota(jnp.int32, (1, width), 1)
    incl = lf
    shift = 1
    while shift < width:
        incl = incl + jnp.where(lane + shift < width, pltpu.roll(incl, width - shift, 1), 0.0)
        shift *= 2
    suffix = carry_ref[:, 0:1] + (incl - lf)
    carry_ref[...] = carry_ref[...] + incl[:, 0:1]
    bias = jnp.zeros((Q_ROWS, width), f32)
    for h in range(H_FOX):
        bias = jnp.where(row_head == h, suffix[h:h + 1, :], bias)
    s = jnp.dot(qbd, kt_s[...], preferred_element_type=f32) + bias + cq
    _online_update(s, vt_s[...], m_ref, l_ref, acc_ref)
    n_half = pg * (PAGE // CMP_STRIDE)
    halves_per_page = PAGE // CMP_STRIDE
    perm = perm_ref[...]
    for kind in range(2 * N_KV_NSA):
        for k in range(pg):
            tile = page_refs[k][0, 0, ROW_CMP + kind].astype(bf16)
            at_ref[kind, k * PAGE:(k + 1) * PAGE, :] = _nt_dot(perm, tile)
        acc = jnp.zeros((n_half, 2 * HEAD_DIM), f32)
        for r in range(CMP_STRIDE):
            rows = jnp.concatenate(
                [at_ref[kind, k * PAGE + r * halves_per_page:k * PAGE + (r + 1) * halves_per_page, :] for k in range(pg)],
                axis=0)
            acc = acc + jnp.dot(rows.astype(bf16), wc_ref[kind, r], preferred_element_type=f32)
        hh_ref[0, kind] = acc

    @pl.when(j == pl.num_programs(1) - 1)
    def _():
        knew = knew_ref[0].astype(bf16)
        s = _nt_dot(qbd, knew)
        trow = lax.shift_right_logical(lax.broadcasted_iota(jnp.int32, (Q_ROWS, 1), 0), 2)
        tcol = lax.broadcasted_iota(jnp.int32, (1, SUBLANES), 1)
        s = jnp.where(tcol <= trow, s + (cq - ck_ref[0]), NEG)
        _online_update(s, vnew_ref[0].T.astype(bf16), m_ref, l_ref, acc_ref)
        o_ref[0] = acc_ref[...] / l_ref[...]


def _decode(l, page_table, qbd, knew, vnew, cq, ck, wc, cache_t, logf_t, pg):
    n, n_pages = page_table.shape
    steps = n_pages // pg

    def page_of(b, j, k, pt):
        return pt[b, (steps - 1 - j) * pg + k]

    per_seq = lambda shp: pl.BlockSpec((1,) + shp, lambda b, j, pt: (b,) + (0,) * len(shp))
    const = lambda shp: pl.BlockSpec(shp, lambda b, j, pt: (0,) * len(shp))
    page_specs = [pl.BlockSpec((1, 1, ROW_SLC, HEAD_DIM, PAGE), lambda b, j, pt, k=k: (l, page_of(b, j, k, pt), 0, 0, 0))
                  for k in range(pg)]
    ii, jj = np.meshgrid(np.arange(PAGE), np.arange(PAGE), indexing="ij")
    halves = PAGE // CMP_STRIDE
    perm = jnp.asarray((jj == CMP_STRIDE * (ii % halves) + ii // halves).astype(np.float32)).astype(bf16)
    lf_specs = [pl.BlockSpec((1, 1, H_FOX, PAGE), lambda b, j, pt, k=k: (l, page_of(b, j, k, pt), 0, 0)) for k in range(pg)]
    n_half = n_pages * (PAGE // CMP_STRIDE)
    grid_spec = pltpu.PrefetchScalarGridSpec(
        num_scalar_prefetch=1,
        grid=(n, steps),
        in_specs=[per_seq((Q_ROWS, FOX_W)), per_seq((SUBLANES, FOX_W)), per_seq((SUBLANES, FOX_W)),
                  per_seq((Q_ROWS, 1)), per_seq((Q_ROWS, SUBLANES)),
                  const(wc.shape)] + page_specs + lf_specs + [const((PAGE, PAGE))],
        out_specs=[per_seq((Q_ROWS, FOX_W)),
                   pl.BlockSpec((1, 2 * N_KV_NSA, pg * (PAGE // CMP_STRIDE), 2 * HEAD_DIM),
                                lambda b, j, pt: (b, 0, steps - 1 - j, 0))],
        scratch_shapes=[pltpu.VMEM((Q_ROWS, 1), f32), pltpu.VMEM((Q_ROWS, 1), f32), pltpu.VMEM((Q_ROWS, FOX_W), f32),
                        pltpu.VMEM((SUBLANES, PAGE), f32), pltpu.VMEM((2 * N_KV_NSA, pg * PAGE, HEAD_DIM), f32),
                        pltpu.VMEM((FOX_W, pg * PAGE), bf16), pltpu.VMEM((FOX_W, pg * PAGE), bf16),
                        pltpu.VMEM((SUBLANES, pg * PAGE), f32)])
    return pl.pallas_call(
        functools.partial(_decode_kernel, pg=pg),
        grid_spec=grid_spec,
        out_shape=[jax.ShapeDtypeStruct((n, Q_ROWS, FOX_W), f32),
                   jax.ShapeDtypeStruct((n, 2 * N_KV_NSA, n_half, 2 * HEAD_DIM), f32)],
        compiler_params=_cparams(("parallel", "arbitrary")),
        name="decode_fox_compress",
    )(page_table, qbd, knew, vnew, cq, ck, wc, *([cache_t] * pg), *([logf_t] * pg), perm)


def _softmax_two(s_list, v_list, sn, vn):
    m = jnp.max(sn, axis=1, keepdims=True)
    for s in s_list:
        m = jnp.maximum(m, jnp.max(s, axis=1, keepdims=True))
    pn = jnp.exp(sn - m)
    den = jnp.sum(pn, axis=1, keepdims=True)
    out = jnp.zeros((sn.shape[0], vn.shape[1]), f32)
    for c in range(vn.shape[0]):
        out = out + pn[:, c:c + 1] * vn[c:c + 1, :]
    for s, v in zip(s_list, v_list):
        p = jnp.exp(s - m)
        den = den + jnp.sum(p, axis=1, keepdims=True)
        out = out + _nt_dot(p.astype(bf16), v)
    return out / den


def _new_token_scores(qg, kn, slope_col, t):
    tcol = lax.broadcasted_iota(jnp.int32, (1, SUBLANES), 1)
    sn = jnp.zeros((SUBLANES, SUBLANES), f32)
    for c in range(SUBLANES):
        sn = jnp.where(tcol == c, jnp.sum(qg * kn[c:c + 1, :], axis=1, keepdims=True), sn)
    return jnp.where(tcol <= t, sn - slope_col * (t - tcol).astype(f32), NEG)


def _nsa_decode_kernel(idx_ref, pt_ref, qs_ref, ksn_ref, vsn_ref, kwn_ref, vwn_ref, win_ref, *rest, past):
    n_k = N_KV_NSA * TOP_N
    k_refs, v_refs = rest[:n_k], rest[n_k:2 * n_k]
    os_ref, ow_ref = rest[2 * n_k:]
    b, t = pl.program_id(0), pl.program_id(1)
    qpos = past + t
    lane = lax.broadcasted_iota(jnp.int32, (1, PAGE), 1)
    lane_half = lax.shift_right_logical(lane, 6)
    rows = lax.broadcasted_iota(jnp.int32, (SUBLANES, 1), 0)
    n_past_blocks = past // SLC_BLOCK
    wcol = lax.broadcasted_iota(jnp.int32, (1, WINDOW), 1)
    for g in range(N_KV_NSA):
        slope_col = jnp.zeros((SUBLANES, 1), f32)
        for r in range(HEADS_PER_GROUP):
            slope_col = jnp.where(rows == r, _alibi_slope(g * HEADS_PER_GROUP + r), slope_col)
        qg = qs_ref[0, 0, g] * 0.125
        qgb = qg.astype(bf16)
        s_list, v_list = [], []
        for k in range(TOP_N):
            blk = idx_ref[b, t, g * TOP_N + k]
            half = lax.rem(blk, 2)
            kpos = (blk // 2) * PAGE + lane
            ok = jnp.logical_and(lane_half == half, blk < n_past_blocks)
            s = jnp.dot(qgb, k_refs[g * TOP_N + k][0, 0, 0].astype(bf16), preferred_element_type=f32)
            s_list.append(jnp.where(ok, s - slope_col * (qpos - kpos).astype(f32), NEG))
            v_list.append(v_refs[g * TOP_N + k][0, 0, 0].astype(bf16))
        sn = _new_token_scores(qg, ksn_ref[0, g], slope_col, t)
        os_ref[0, 0, g] = _softmax_two(s_list, v_list, sn, vsn_ref[0, g])
        d = WINDOW + t - wcol
        sw = jnp.dot(qgb, win_ref[0, 0, g].astype(bf16), preferred_element_type=f32)
        sw = jnp.where(d <= WINDOW, sw - slope_col * d.astype(f32), NEG)
        swn = _new_token_scores(qg, kwn_ref[0, g], slope_col, t)
        ow_ref[0, 0, g] = _softmax_two([sw], [win_ref[0, 0, N_KV_NSA + g].astype(bf16)], swn, vwn_ref[0, g])


def _nsa_decode(l, idx, page_table, qs, ksn, vsn, kwn, vwn, win_t, cache_t, past):
    n, t_new = idx.shape[0], idx.shape[1]
    n_pages = page_table.shape[1]

    def sel_map(g, k, row0):
        def f(b, t, idx_r, pt_r):
            page = jnp.minimum(idx_r[b, t, g * TOP_N + k] // 2, n_pages - 1)
            return (l, pt_r[b, page], row0 + g, 0, 0)
        return f

    tile = (1, 1, 1, HEAD_DIM, PAGE)
    k_specs = [pl.BlockSpec(tile, sel_map(g, k, ROW_SLC)) for g in range(N_KV_NSA) for k in range(TOP_N)]
    v_specs = [pl.BlockSpec(tile, sel_map(g, k, ROW_SLC + N_KV_NSA)) for g in range(N_KV_NSA) for k in range(TOP_N)]
    new_spec = pl.BlockSpec((1, N_KV_NSA, SUBLANES, HEAD_DIM), lambda b, t, i, p: (b, 0, 0, 0))
    q_spec = pl.BlockSpec((1, 1, N_KV_NSA, SUBLANES, HEAD_DIM), lambda b, t, i, p: (b, t, 0, 0, 0))
    grid_spec = pltpu.PrefetchScalarGridSpec(
        num_scalar_prefetch=2,
        grid=(n, t_new),
        in_specs=[q_spec, new_spec, new_spec, new_spec, new_spec,
                  pl.BlockSpec((1, 1, 2 * N_KV_NSA, HEAD_DIM, WINDOW), lambda b, t, i, p: (l, b, 0, 0, 0))] + k_specs + v_specs,
        out_specs=[q_spec, q_spec])
    n_k = N_KV_NSA * TOP_N
    shp = jax.ShapeDtypeStruct((n, t_new, N_KV_NSA, SUBLANES, HEAD_DIM), f32)
    return pl.pallas_call(
        functools.partial(_nsa_decode_kernel, past=past),
        grid_spec=grid_spec,
        out_shape=[shp, shp],
        compiler_params=_cparams(("parallel", "arbitrary")),
        name="decode_slc_win",
    )(idx, page_table, qs, ksn, vsn, kwn, vwn, win_t, *([cache_t] * (2 * n_k)))


def _conv_sample_kernel(b_ref, c_ref, u_ref, w_ref, o_ref, un_ref):
    u = c_ref[...] * u_ref[...]
    w = w_ref[...]
    y = w[0:1] * pltpu.roll(u, 2, 0) + w[1:2] * pltpu.roll(u, 1, 0) + w[2:3] * u
    o_ref[...] = b_ref[...] * y
    un_ref[...] = u


def _conv_sample(b_ext, c_ext, u_ext, w8):
    shp = jax.ShapeDtypeStruct(b_ext.shape, f32)
    return pl.pallas_call(_conv_sample_kernel, out_shape=[shp, shp], name="conv_sample")(b_ext, c_ext, u_ext, w8)


def _sample_mixer(x, lw, l, cache_t, logf_t, win_t, conv_state, page_table, past, n, t_new, pg):
    pk = lw["pk"]
    m = n * t_new
    assert t_new * H_FOX == Q_ROWS and m == LANES and past % PAGE == 0 and past >= WINDOW
    pr, pt = _project(x, pk["wr_all"], pk["wt"], m)
    lf_t, cum_t = _logf_cumsum(pt, pk["bf8"], m, segment=t_new)
    seq = lambda a: a.reshape(n, t_new, -1)
    pad8 = lambda a: jnp.pad(a, ((0, 0), (0, SUBLANES - t_new), (0, 0)))
    kv_rows = pr[:, R_KV:R_WIN]
    win_rows = pr[:, R_WIN:R_ALL_END]
    head_mask = jnp.asarray((np.arange(FOX_W)[None, :] // HEAD_DIM == np.arange(H_FOX)[:, None]).astype(np.float32))
    qbd = (seq(pr[:, R_QF:R_QF + FOX_W])[:, :, None, :] * head_mask[None, None]).reshape(n, Q_ROWS, FOX_W)
    knew, vnew = pad8(seq(kv_rows[:, 0:FOX_W])), pad8(seq(kv_rows[:, FOX_W:2 * FOX_W]))
    cum = cum_t[:H_FOX].reshape(H_FOX, n, t_new)
    cq = jnp.transpose(cum, (1, 2, 0)).reshape(n, Q_ROWS, 1)
    ck = jnp.broadcast_to(jnp.transpose(cum, (1, 0, 2))[:, None], (n, t_new, H_FOX, t_new)).reshape(n, Q_ROWS, t_new)
    ck = jnp.pad(ck, ((0, 0), (0, 0), (0, SUBLANES - t_new)))
    o16, hh = _decode(l, page_table, qbd, knew, vnew, cq, ck, pk["wc"], cache_t, logf_t, pg)
    o16 = o16.reshape(n, t_new, H_FOX, H_FOX, HEAD_DIM)
    o_fox = jnp.stack([o16[:, :, h, h] for h in range(H_FOX)], axis=2).reshape(m, FOX_W)
    qn = pr[:, R_QN:R_QN + H_NSA * HEAD_DIM]
    q8 = pad8(seq(qn)).reshape(n * SUBLANES, H_NSA * HEAD_DIM)
    o_c8, _, idx8, _ = _cmp_select(q8, 0, hh, past, t_new, SUBLANES)
    o_c = o_c8.reshape(n, SUBLANES, -1)[:, :t_new].reshape(m, -1)
    idx = idx8.reshape(n, SUBLANES, LANES)[:, :t_new, :N_KV_NSA * TOP_N]
    qs = qn.reshape(n, t_new, N_KV_NSA, HEADS_PER_GROUP, HEAD_DIM)
    qs = jnp.pad(qs, ((0, 0), (0, 0), (0, 0), (0, SUBLANES - HEADS_PER_GROUP), (0, 0)))
    grp = lambda a: jnp.pad(jnp.transpose(a.reshape(n, t_new, N_KV_NSA, HEAD_DIM), (0, 2, 1, 3)),
                            ((0, 0), (0, 0), (0, SUBLANES - t_new), (0, 0)))
    gw = N_KV_NSA * HEAD_DIM
    ksn, vsn = grp(kv_rows[:, 6 * gw:7 * gw]), grp(kv_rows[:, 7 * gw:8 * gw])
    kwn, vwn = grp(win_rows[:, 0:gw]), grp(win_rows[:, gw:2 * gw])
    o_s5, o_w5 = _nsa_decode(l, idx, page_table, qs, ksn, vsn, kwn, vwn, win_t, cache_t, past)
    heads = lambda a: a[:, :, :, :HEADS_PER_GROUP].reshape(m, H_NSA * HEAD_DIM)
    o_s, o_w = heads(o_s5), heads(o_w5)
    ext = lambda a, head: jnp.concatenate([head, seq(a)], axis=1).reshape(n * SUBLANES, D_CONV)
    zeros2 = jnp.zeros((n, 2, D_CONV), f32)
    one_head = jnp.concatenate([zeros2, jnp.ones((n, 2, D_CONV), f32)], axis=1)
    b_ext = ext(pr[:, R_BG:R_BG + D_CONV], jnp.zeros((n, 4, D_CONV), f32))
    c_ext = ext(pr[:, R_CG:R_CG + D_CONV], one_head)
    u_ext = ext(pr[:, R_UIN:R_UIN + D_CONV], jnp.concatenate([zeros2, conv_state], axis=1))
    o_conv8, u8 = _conv_sample(b_ext, c_ext, u_ext, pk["w8"])
    o_conv = o_conv8.reshape(n, SUBLANES, D_CONV)[:, SUBLANES - t_new:].reshape(m, D_CONV)
    new_conv = u8.reshape(n, SUBLANES, D_CONV)[:, SUBLANES - (CONV_K - 1):]
    new_kv = kv_rows.reshape(n, t_new, N_KV_FEAT // HEAD_DIM, HEAD_DIM)
    new_logf = jnp.transpose(lf_t[:H_FOX].reshape(H_FOX, n, t_new), (1, 2, 0))
    win_new_t = jnp.transpose(pt[T_KW:T_FL].reshape(2 * N_KV_NSA, HEAD_DIM, n, t_new), (2, 0, 1, 3))
    new_win_t = jnp.concatenate([win_t[l][..., t_new:], win_new_t], axis=-1)
    return pr, o_fox, o_conv, o_c, o_s, o_w, new_kv, new_logf, new_conv, new_win_t


def _trunk_sample(x, layers, alpha, cache_t, logf_t, win_t, state_conv, page_table, tiles):
    n, t_new, d = x.shape
    past = page_table.shape[1] * PAGE
    x = x.reshape(n * t_new, d)
    m = n * t_new
    pg = min(tiles["pg"], page_table.shape[1])
    kv, logf, conv, win = [], [], [], []
    for l, lw in enumerate(layers):
        pr, o_fox, o_conv, o_c, o_s, o_w, nkv, nlogf, nconv, nwin_t = _sample_mixer(
            x, lw, l, cache_t, logf_t, win_t, state_conv[l], page_table, past, n, t_new, pg)
        x = _outproj_ln(x, o_fox, o_conv, o_c, o_s, o_w, pr, R_SM, lw["w_out"], lw["ln1_g"], lw["ln1_b"], alpha, m)
        if lw["moe"] is None:
            w1, w3, w2 = lw["dense"]
            x = _ffn_ln(x, w1, w3, w2, lw["ln2_g"], lw["ln2_b"], alpha, m, tiles["tf"])
        else:
            rw, w1, w3, w2 = lw["moe"]
            x = _moe_ln(x, rw, w1, w3, w2, lw["ln2_g"], lw["ln2_b"], alpha, m, tiles["tf"])
        kv.append(nkv)
        logf.append(nlogf)
        conv.append(nconv)
        win.append(jnp.transpose(nwin_t, (0, 3, 1, 2)))
    return x.reshape(n, t_new, d), jnp.stack(kv), jnp.stack(logf), jnp.stack(conv), jnp.stack(win)


def kernel(x_prompt, x_sample, cache_kv, cache_logf, state_conv, state_win, page_table, w_in, b_f, conv_w, w_cmp_k, w_cmp_v, w_out, ln1_g, ln1_b, ln2_g, ln2_b, dense_w1, dense_w3, dense_w2, router_w, moe_w1, moe_w3, moe_w2):
    depth = w_in.shape[0]
    alpha = (2.0 * depth) ** 0.25
    layers = _prepare_layers(w_in, b_f, conv_w, w_cmp_k, w_cmp_v, w_out, ln1_g, ln1_b, ln2_g, ln2_b,
                             dense_w1, dense_w3, dense_w2, router_w, moe_w1, moe_w3, moe_w2)
    y_p, kv_p, logf_p, conv_p, win_p = _trunk_prompt(x_prompt[0], layers, alpha, TILES)
    cache_t = jnp.transpose(cache_kv, (0, 1, 3, 4, 2))
    logf_t = jnp.transpose(cache_logf, (0, 1, 3, 2))
    win_t = jnp.transpose(state_win, (0, 1, 3, 4, 2))
    y_s, kv_s, logf_s, conv_s, win_s = _trunk_sample(
        x_sample, layers, alpha, cache_t, logf_t, win_t, state_conv, page_table, TILES)
    return (y_p[None], y_s, kv_p, logf_p, conv_p, win_p, kv_s, logf_s, conv_s, win_s)
```

```python
import functools

import numpy as np
import jax
import jax.numpy as jnp
from jax import lax
from jax.experimental import pallas as pl
from jax.experimental.pallas import tpu as pltpu

f32 = jnp.float32
bf16 = jnp.bfloat16

HEAD_DIM = 64
H_FOX = 4
D_CONV = 256
H_NSA = 8
N_KV_NSA = 2
HEADS_PER_GROUP = H_NSA // N_KV_NSA
CONV_K = 3
CMP_STRIDE = 16
SLC_BLOCK = 64
TOP_N = 16
N_LOCAL_BLOCKS = 2
SLC_COVER_W = (1.0, 2.0, 2.0, 2.0, 1.0)
WINDOW = 512
N_EXPERTS = 8
TOP_K = 2
LN_EPS = 1e-5
FORCE_BONUS = 1e4
NEG = -1e30
LANES = 128
SUBLANES = 8
VMEM_LIMIT = 56 * 1024 * 1024
HIGHEST = lax.Precision.HIGHEST

_C_QF, _C_KF, _C_VF, _C_FL = 0, 256, 512, 768
_C_BG, _C_CG, _C_UIN, _C_QN = 772, 1028, 1284, 1540
_C_KC, _C_VC, _C_KS, _C_VS, _C_KW, _C_VW, _C_GN, _C_END = 2052, 2180, 2308, 2436, 2564, 2692, 2820, 2844
R_QN, R_QF, R_BG, R_CG, R_UIN, R_SM, R_END = 0, 512, 768, 1024, 1280, 1536, 1664
SM_GN = 4
T_KF, T_VF, T_KC, T_VC, T_KS, T_VS, T_KW, T_VW, T_FL, T_END = 0, 256, 512, 640, 768, 896, 1024, 1152, 1280, 1288
N_KV_FEAT = 1024


def _cparams(sem):
    return pltpu.CompilerParams(dimension_semantics=sem, vmem_limit_bytes=VMEM_LIMIT)


def _nt_dot(a, b):
    return lax.dot_general(a, b, (((1,), (1,)), ((), ())), preferred_element_type=f32)


def _log_sigmoid(x):
    return -(jnp.maximum(-x, 0.0) + jnp.log1p(jnp.exp(-jnp.abs(x))))


def _sigmoid(x):
    return 1.0 / (1.0 + jnp.exp(-x))


def _proj_kernel(x_ref, wr_ref, wt_ref, pr_ref, pt_ref):
    xb = x_ref[...].astype(bf16)
    pr_ref[...] = jnp.dot(xb, wr_ref[...], preferred_element_type=f32)
    pt_ref[...] = _nt_dot(wt_ref[...], xb)


def _project(x, wr, wt, tm):
    m, d = x.shape
    cr, ct = wr.shape[1], wt.shape[0]
    return pl.pallas_call(
        _proj_kernel,
        grid=(m // tm,),
        in_specs=[pl.BlockSpec((tm, d), lambda i: (i, 0)),
                  pl.BlockSpec((d, cr), lambda i: (0, 0)),
                  pl.BlockSpec((ct, d), lambda i: (0, 0))],
        out_specs=[pl.BlockSpec((tm, cr), lambda i: (i, 0)),
                   pl.BlockSpec((ct, tm), lambda i: (0, i))],
        out_shape=[jax.ShapeDtypeStruct((m, cr), f32), jax.ShapeDtypeStruct((ct, m), f32)],
        compiler_params=_cparams(("parallel",)),
        name="project",
    )(x, wr, wt)


def _logf_cumsum_kernel(f_ref, bf_ref, tri_ref, logf_ref, c_ref, carry_ref):
    @pl.when(pl.program_id(0) == 0)
    def _():
        carry_ref[...] = jnp.zeros_like(carry_ref)

    carry = carry_ref[...]
    n_sub = f_ref.shape[1] // LANES
    for s in range(n_sub):
        sl = slice(s * LANES, (s + 1) * LANES)
        lf = _log_sigmoid(f_ref[:, sl] + bf_ref[...])
        logf_ref[:, sl] = lf
        cs = jnp.dot(lf, tri_ref[...], precision=HIGHEST, preferred_element_type=f32) + carry
        c_ref[:, sl] = cs
        carry = jnp.broadcast_to(cs[:, LANES - 1:LANES], carry.shape)
    carry_ref[...] = carry


def _logf_cumsum(pt, bf8, tc, segment=LANES):
    t = pt.shape[1]
    ii, jj = np.meshgrid(np.arange(LANES), np.arange(LANES), indexing="ij")
    tri = jnp.asarray(((ii <= jj) & (ii // segment == jj // segment)).astype(np.float32))
    return pl.pallas_call(
        _logf_cumsum_kernel,
        grid=(t // tc,),
        in_specs=[pl.BlockSpec((SUBLANES, tc), lambda j: (T_FL // SUBLANES, j)),
                  pl.BlockSpec((SUBLANES, 1), lambda j: (0, 0)),
                  pl.BlockSpec((LANES, LANES), lambda j: (0, 0))],
        out_specs=[pl.BlockSpec((SUBLANES, tc), lambda j: (0, j)),
                   pl.BlockSpec((SUBLANES, tc), lambda j: (0, j))],
        out_shape=[jax.ShapeDtypeStruct((SUBLANES, t), f32)] * 2,
        scratch_shapes=[pltpu.VMEM((SUBLANES, LANES), f32)],
        compiler_params=_cparams(("arbitrary",)),
        name="logf_cumsum",
    )(pt, bf8, tri)


def _alibi_slope(h):
    return float(2.0 ** (-8.0 * (h + 1) / H_NSA))


def _flash_kernel(*refs, mode, n_heads, tq, tk):
    if mode == "fox":
        flags_ref, q_ref, kt_ref, vt_ref, c_ref, cq_ref, o_ref, m_ref, l_ref, acc_ref = refs
    elif mode == "slc":
        flags_ref, q_ref, kt_ref, vt_ref, sel_ref, o_ref, m_ref, l_ref, acc_ref = refs
    else:
        q_ref, kt_ref, vt_ref, o_ref, m_ref, l_ref, acc_ref = refs
    i, j = pl.program_id(0), pl.program_id(1)
    nq, nk = pl.num_programs(0), pl.num_programs(1)

    @pl.when(j == 0)
    def _():
        m_ref[...] = jnp.full_like(m_ref, NEG)
        l_ref[...] = jnp.zeros_like(l_ref)
        acc_ref[...] = jnp.zeros_like(acc_ref)

    if mode == "win":
        kv_tile = jnp.maximum(i - 1 + j, 0)
        active = jnp.logical_or(i > 0, j > 0)
    elif mode == "fox":
        kv_tile = jnp.maximum(i - j, 0)
        active = j <= i
    else:
        kv_tile = j
        active = j <= i
    q0 = i * tq

    def positions():
        qpos = q0 + lax.broadcasted_iota(jnp.int32, (tq, 1), 0)
        kpos = kv_tile * tk + lax.broadcasted_iota(jnp.int32, (1, tk), 1)
        return qpos, kpos

    def attend(h, kv, bias, mask):
        qh = (q_ref[:, h * HEAD_DIM:(h + 1) * HEAD_DIM] * 0.125).astype(bf16)
        kth = kt_ref[kv * HEAD_DIM:(kv + 1) * HEAD_DIM, :].astype(bf16)
        vth = vt_ref[kv * HEAD_DIM:(kv + 1) * HEAD_DIM, :].astype(bf16)
        s = jnp.dot(qh, kth, preferred_element_type=f32) + bias + mask
        m_old = m_ref[h]
        m_new = jnp.maximum(m_old, jnp.max(s, axis=1, keepdims=True))
        alpha = jnp.exp(m_old - m_new)
        p = jnp.exp(s - m_new)
        l_ref[h] = alpha * l_ref[h] + jnp.sum(p, axis=1, keepdims=True)
        acc_ref[h] = alpha * acc_ref[h] + _nt_dot(p.astype(bf16), vth)
        m_ref[h] = m_new

    if mode == "fox":
        @pl.when(jnp.logical_and(active, flags_ref[i * nk + kv_tile] > 0))
        def _():
            qpos, kpos = positions()
            neg = jnp.where(kpos > qpos, NEG, 0.0)
            for h in range(n_heads):
                attend(h, h, cq_ref[h:h + 1, 0:1] - c_ref[h:h + 1, :], neg)
    elif mode == "slc":
        nbs = sel_ref.shape[1] // N_KV_NSA
        for g in range(N_KV_NSA):
            @pl.when(jnp.logical_and(active, flags_ref[(g * nq + i) * nk + kv_tile] > 0))
            def _(g=g):
                qpos, kpos = positions()
                rel = (kpos - q0).astype(f32)
                blk_of_key = lax.shift_right_logical(kpos, 6)
                expand = (lax.broadcasted_iota(jnp.int32, (nbs, 1), 0) == blk_of_key).astype(bf16)
                selg = sel_ref[:, g * nbs:(g + 1) * nbs].astype(bf16)
                hit = jnp.dot(selg, expand, preferred_element_type=f32)
                mask = jnp.where(jnp.logical_and(hit > 0.5, kpos <= qpos), 0.0, NEG)
                for r in range(HEADS_PER_GROUP):
                    h = g * HEADS_PER_GROUP + r
                    attend(h, g, _alibi_slope(h) * rel, mask)
    else:
        @pl.when(active)
        def _():
            qpos, kpos = positions()
            d = qpos - kpos
            neg = jnp.where(jnp.logical_or(d < 0, d > WINDOW), NEG, 0.0)
            rel = (kpos - q0).astype(f32)
            for h in range(n_heads):
                attend(h, h // HEADS_PER_GROUP, _alibi_slope(h) * rel, neg)

    @pl.when(j == nk - 1)
    def _():
        for h in range(n_heads):
            o_ref[:, h * HEAD_DIM:(h + 1) * HEAD_DIM] = acc_ref[h] / l_ref[h]


EXP_DEAD_MARGIN = 110.0


def _fox_tile_flags(pr, pt, c_t, tq):
    t = pr.shape[0]
    nq = t // tq
    q = pr[:, R_QF:R_QF + H_FOX * HEAD_DIM].reshape(nq, tq, H_FOX, HEAD_DIM)
    qn = jnp.sqrt(jnp.max(jnp.sum(q * q, axis=-1), axis=1))
    k = pt[T_KF:T_KF + H_FOX * HEAD_DIM].reshape(H_FOX, HEAD_DIM, nq, tq)
    kn = jnp.sqrt(jnp.max(jnp.sum(k * k, axis=1), axis=-1)).T
    c4 = c_t[:H_FOX].reshape(H_FOX, nq, tq)
    cq0, cend = c4[:, :, 0].T, c4[:, :, -1].T
    rounding = 1.01
    ub = rounding * 0.125 * qn[:, None, :] * kn[None, :, :] + (cq0[:, None, :] - cend[None, :, :])
    self_lb = -rounding * 0.125 * qn * kn
    dead = jnp.all(ub - self_lb[:, None, :] < -EXP_DEAD_MARGIN, axis=-1)
    live = jnp.logical_or(jnp.logical_not(dead), jnp.eye(nq, dtype=bool))
    return live.reshape(-1).astype(jnp.int32)


def _slc_tile_flags(anyb, tq, tq_cmp, t):
    nq = t // tq
    nbs = anyb.shape[1] // N_KV_NSA
    per_kv = tq // SLC_BLOCK
    a = anyb[::SUBLANES].reshape(nq, tq // tq_cmp, N_KV_NSA, nbs // per_kv, per_kv)
    flags = (jnp.max(a, axis=(1, 4)) > 0.5)[:, :, :nq]
    return jnp.transpose(flags, (1, 0, 2)).reshape(-1).astype(jnp.int32)


def _flash(mode, pr, pt, q_col, kt_row, vt_row, tq, extra=()):
    t = pr.shape[0]
    n_heads = H_FOX if mode == "fox" else H_NSA
    n_kv = H_FOX if mode == "fox" else N_KV_NSA
    qw, kw = n_heads * HEAD_DIM, n_kv * HEAD_DIM
    tk = tq
    nq = t // tq
    nk = 2 if mode == "win" else nq
    if mode == "win":
        kv_map = lambda i, j, *_: jnp.maximum(i - 1 + j, 0)
    elif mode == "fox":
        def kv_map(i, j, flags):
            kv = jnp.maximum(i - j, 0)
            return jnp.where(flags[i * nk + kv] > 0, kv, i)
    elif mode == "slc":
        def kv_map(i, j, flags):
            kv = jnp.minimum(j, i)
            used = jnp.logical_or(flags[i * nk + kv] > 0, flags[(nq + i) * nk + kv] > 0)
            return jnp.where(used, kv, 0)
    in_specs = [pl.BlockSpec((tq, qw), lambda i, j, *_: (i, q_col // qw)),
                pl.BlockSpec((kw, tk), lambda i, j, *p: (kt_row // kw, kv_map(i, j, *p))),
                pl.BlockSpec((kw, tk), lambda i, j, *p: (vt_row // kw, kv_map(i, j, *p)))]
    prefetch = ()
    if mode == "fox":
        (ct,) = extra
        in_specs += [pl.BlockSpec((SUBLANES, tk), lambda i, j, *p: (0, kv_map(i, j, *p))),
                     pl.BlockSpec((SUBLANES, LANES), lambda i, j, *_: (0, i * (tq // LANES)))]
        args = (pr, pt, pt, ct, ct)
        prefetch = (_fox_tile_flags(pr, pt, ct, tq),)
    elif mode == "slc":
        sel, flags = extra
        in_specs += [pl.BlockSpec((tq, sel.shape[1]), lambda i, j, *_: (i, 0))]
        args = (pr, pt, pt, sel)
        prefetch = (flags,)
    else:
        args = (pr, pt, pt)
    grid_spec = pltpu.PrefetchScalarGridSpec(
        num_scalar_prefetch=len(prefetch),
        grid=(nq, nk),
        in_specs=in_specs,
        out_specs=pl.BlockSpec((tq, qw), lambda i, j, *_: (i, 0)),
        scratch_shapes=[pltpu.VMEM((n_heads, tq, 1), f32), pltpu.VMEM((n_heads, tq, 1), f32),
                        pltpu.VMEM((n_heads, tq, HEAD_DIM), f32)])
    return pl.pallas_call(
        functools.partial(_flash_kernel, mode=mode, n_heads=n_heads, tq=tq, tk=tk),
        grid_spec=grid_spec,
        out_shape=jax.ShapeDtypeStruct((t, qw), f32),
        compiler_params=_cparams(("parallel", "arbitrary")),
        name="flash_" + mode,
    )(*prefetch, *args)


def _compress_tile(a, at_ref, wc):
    tb = a.shape[1]
    at_ref[...] = a.T
    n_half = tb // CMP_STRIDE
    acc = jnp.zeros((n_half, 2 * HEAD_DIM), f32)
    for r in range(CMP_STRIDE):
        rows = at_ref[pl.ds(r, n_half, stride=CMP_STRIDE), :]
        acc = acc + jnp.dot(rows.astype(bf16), wc(r), preferred_element_type=f32)
    return acc


def _compress_kernel(a_ref, wc_ref, o_ref, at_ref):
    o_ref[0] = _compress_tile(a_ref[...], at_ref, lambda r: wc_ref[0, r])


def _compress(pt, wc, tb):
    t = pt.shape[1]
    base = T_KC // HEAD_DIM
    return pl.pallas_call(
        _compress_kernel,
        grid=(2 * N_KV_NSA, t // tb),
        in_specs=[pl.BlockSpec((HEAD_DIM, tb), lambda k, j: (base + k, j)),
                  pl.BlockSpec((1, CMP_STRIDE, HEAD_DIM, 2 * HEAD_DIM), lambda k, j: (k, 0, 0, 0))],
        out_specs=pl.BlockSpec((1, tb // CMP_STRIDE, 2 * HEAD_DIM), lambda k, j: (k, j, 0)),
        out_shape=jax.ShapeDtypeStruct((2 * N_KV_NSA, t // CMP_STRIDE, 2 * HEAD_DIM), f32),
        scratch_shapes=[pltpu.VMEM((tb, HEAD_DIM), f32)],
        compiler_params=_cparams(("parallel", "parallel")),
        name="compress",
    )(pt, wc)


def _cmp_select_kernel(q_ref, hh_ref, cov_ref, oc_ref, sel_ref, idx_ref, any_ref, *, past, tq, nbs):
    n_half = hh_ref.shape[2]
    nbsp = cov_ref.shape[1]
    t0 = pl.program_id(1) * tq
    qpos = past + t0 + lax.broadcasted_iota(jnp.int32, (tq, 1), 0)
    cmp_end = lax.broadcasted_iota(jnp.int32, (1, n_half), 1) * CMP_STRIDE + (2 * CMP_STRIDE - 1)
    valid_c = cmp_end <= qpos
    dist = (qpos - cmp_end).astype(f32)
    blk = lax.broadcasted_iota(jnp.int32, (1, nbsp), 1)
    blk_f = blk.astype(f32)
    cur = lax.shift_right_logical(qpos, 6)
    valid_s = blk <= cur
    forced = jnp.logical_or(blk == 0, blk >= cur - (N_LOCAL_BLOCKS - 1))
    lane = lax.broadcasted_iota(jnp.int32, (1, LANES), 1)
    idx_acc = jnp.zeros((tq, LANES), f32)
    scores = []
    for g in range(N_KV_NSA):
        hk = hh_ref[0, g]
        hv = hh_ref[0, N_KV_NSA + g]
        ck = hk[:, :HEAD_DIM] + pltpu.roll(hk, n_half - 1, 0)[:, HEAD_DIM:]
        cv = hv[:, :HEAD_DIM] + pltpu.roll(hv, n_half - 1, 0)[:, HEAD_DIM:]
        ckb, cvb = ck.astype(bf16), cv.astype(bf16)
        imp = jnp.zeros((tq, n_half), f32)
        for r in range(HEADS_PER_GROUP):
            h = g * HEADS_PER_GROUP + r
            qh = (q_ref[:, h * HEAD_DIM:(h + 1) * HEAD_DIM] * 0.125).astype(bf16)
            s = _nt_dot(qh, ckb) - _alibi_slope(h) * dist
            s = jnp.where(valid_c, s, -jnp.inf)
            m = jnp.max(s, axis=1, keepdims=True)
            m = jnp.where(m == -jnp.inf, 0.0, m)
            e = jnp.exp(s - m)
            p = e / jnp.maximum(jnp.sum(e, axis=1, keepdims=True), 1e-30)
            oc_ref[:, h * HEAD_DIM:(h + 1) * HEAD_DIM] = jnp.dot(p.astype(bf16), cvb, preferred_element_type=f32)
            imp = imp + p
        score = jnp.zeros((tq, nbsp), f32)
        rest = imp
        for _ in range(3):
            piece = rest.astype(bf16)
            score = score + jnp.dot(piece, cov_ref[...], preferred_element_type=f32)
            rest = rest - piece.astype(f32)
        score = jnp.where(valid_s, score + jnp.where(forced, FORCE_BONUS, 0.0), -FORCE_BONUS)
        scores.append(jnp.where(blk < nbs, score, -jnp.inf))
    selm = [jnp.zeros((tq, nbsp), f32) for _ in range(N_KV_NSA)]
    for k in range(min(TOP_N, nbs)):
        for g in range(N_KV_NSA):
            mx = jnp.max(scores[g], axis=1, keepdims=True)
            ix = jnp.min(jnp.where(scores[g] == mx, blk_f, 1e9), axis=1, keepdims=True)
            hit = blk_f == ix
            selm[g] = jnp.where(hit, 1.0, selm[g])
            scores[g] = jnp.where(hit, -jnp.inf, scores[g])
            idx_acc = jnp.where(lane == g * TOP_N + k, ix, idx_acc)
    for g in range(N_KV_NSA):
        sel_ref[:, g * nbsp:(g + 1) * nbsp] = selm[g]
        any_ref[:, g * nbsp:(g + 1) * nbsp] = jnp.broadcast_to(jnp.max(selm[g], axis=0, keepdims=True), (SUBLANES, nbsp))
    idx_ref[...] = idx_acc.astype(jnp.int32)


def _cover_matrix(n_half, nbs, nbsp):
    cov = np.zeros((n_half, nbsp), np.float32)
    ratio = SLC_BLOCK // CMP_STRIDE
    for jb in range(nbs):
        for mm, w in enumerate(SLC_COVER_W):
            c = ratio * jb + mm - 1
            if 0 <= c < n_half - 1:
                cov[c, jb] += w
    return jnp.asarray(cov).astype(bf16)


def _cmp_select(q, q_col, hh, past, t_len, tq):
    hh = hh.reshape(-1, 2 * N_KV_NSA, hh.shape[-2], hh.shape[-1])
    n = hh.shape[0]
    n_half = hh.shape[2]
    tp = q.shape[0] // n
    nbs = -(-(past + t_len) // SLC_BLOCK)
    nbsp = -(-nbs // LANES) * LANES
    cov = _cover_matrix(n_half, nbs, nbsp)
    qw = H_NSA * HEAD_DIM
    hh4 = hh
    nt = tp // tq
    return pl.pallas_call(
        functools.partial(_cmp_select_kernel, past=past, tq=tq, nbs=nbs),
        grid=(n, nt),
        in_specs=[pl.BlockSpec((tq, qw), lambda b, i: (b * nt + i, q_col // qw)),
                  pl.BlockSpec((1, 2 * N_KV_NSA, n_half, 2 * HEAD_DIM), lambda b, i: (b, 0, 0, 0)),
                  pl.BlockSpec((n_half, nbsp), lambda b, i: (0, 0))],
        out_specs=[pl.BlockSpec((tq, qw), lambda b, i: (b * nt + i, 0)),
                   pl.BlockSpec((tq, N_KV_NSA * nbsp), lambda b, i: (b * nt + i, 0)),
                   pl.BlockSpec((tq, LANES), lambda b, i: (b * nt + i, 0)),
                   pl.BlockSpec((SUBLANES, N_KV_NSA * nbsp), lambda b, i: (b * nt + i, 0))],
        out_shape=[jax.ShapeDtypeStruct((n * tp, qw), f32),
                   jax.ShapeDtypeStruct((n * tp, N_KV_NSA * nbsp), f32),
                   jax.ShapeDtypeStruct((n * tp, LANES), jnp.int32),
                   jax.ShapeDtypeStruct((n * nt * SUBLANES, N_KV_NSA * nbsp), f32)],
        compiler_params=_cparams(("parallel", "parallel")),
        name="cmp_select",
    )(q, hh4, cov)


def _conv_kernel(b_ref, c_ref, u_ref, ch_ref, uh_ref, st_ref, w_ref, o_ref, last_ref):
    i = pl.program_id(0)
    u = c_ref[...] * u_ref[...]
    halo = jnp.where(i == 0, st_ref[...], ch_ref[...] * uh_ref[...])
    ue = jnp.concatenate([halo, u], axis=0)
    u1 = pltpu.roll(ue, 1, 0)[SUBLANES:]
    u2 = pltpu.roll(ue, 2, 0)[SUBLANES:]
    w = w_ref[...]
    y = w[0:1] * u2 + w[1:2] * u1 + w[2:3] * u
    o_ref[...] = b_ref[...] * y
    last_ref[...] = u[u.shape[0] - SUBLANES:]


def _conv_prompt(pr, state8, w8, tm):
    t = pr.shape[0]
    cb = lambda off: off // D_CONV
    hb = tm // SUBLANES
    return pl.pallas_call(
        _conv_kernel,
        grid=(t // tm,),
        in_specs=[pl.BlockSpec((tm, D_CONV), lambda i: (i, cb(R_BG))),
                  pl.BlockSpec((tm, D_CONV), lambda i: (i, cb(R_CG))),
                  pl.BlockSpec((tm, D_CONV), lambda i: (i, cb(R_UIN))),
                  pl.BlockSpec((SUBLANES, D_CONV), lambda i: (jnp.maximum(i * hb - 1, 0), cb(R_CG))),
                  pl.BlockSpec((SUBLANES, D_CONV), lambda i: (jnp.maximum(i * hb - 1, 0), cb(R_UIN))),
                  pl.BlockSpec((SUBLANES, D_CONV), lambda i: (0, 0)),
                  pl.BlockSpec((SUBLANES, D_CONV), lambda i: (0, 0))],
        out_specs=[pl.BlockSpec((tm, D_CONV), lambda i: (i, 0)),
                   pl.BlockSpec((SUBLANES, D_CONV), lambda i: (0, 0))],
        out_shape=[jax.ShapeDtypeStruct((t, D_CONV), f32), jax.ShapeDtypeStruct((SUBLANES, D_CONV), f32)],
        compiler_params=_cparams(("arbitrary",)),
        name="conv",
    )(pr, pr, pr, pr, pr, state8, w8)


def _layernorm(z, g, b):
    mu = jnp.mean(z, axis=-1, keepdims=True)
    zc = z - mu
    var = jnp.mean(zc * zc, axis=-1, keepdims=True)
    return zc * lax.rsqrt(var + LN_EPS) * g + b


def _outproj_kernel(x_ref, fox_ref, conv_ref, oc_ref, os_ref, ow_ref, sm_ref, ex_ref, w_ref, g_ref, b_ref,
                    o_ref, *, alpha):
    sig = _sigmoid(sm_ref[...])
    nsa = None
    for br, br_ref in enumerate((oc_ref, os_ref, ow_ref)):
        gate = jnp.dot(sig, ex_ref[br], precision=HIGHEST, preferred_element_type=f32)
        term = gate * br_ref[...]
        nsa = term if nsa is None else nsa + term
    a0, a1 = H_FOX * HEAD_DIM, H_FOX * HEAD_DIM + D_CONV
    mix = jnp.dot(fox_ref[...].astype(bf16), w_ref[0:a0, :], preferred_element_type=f32)
    mix = mix + jnp.dot(conv_ref[...].astype(bf16), w_ref[a0:a1, :], preferred_element_type=f32)
    mix = mix + jnp.dot(nsa.astype(bf16), w_ref[a1:, :], preferred_element_type=f32)
    o_ref[...] = _layernorm(alpha * x_ref[...] + mix, g_ref[...], b_ref[...])


def _gate_expand():
    ex = np.zeros((3, LANES, H_NSA * HEAD_DIM), np.float32)
    for h in range(H_NSA):
        for br in range(3):
            ex[br, SM_GN + h * 3 + br, h * HEAD_DIM:(h + 1) * HEAD_DIM] = 1.0
    return jnp.asarray(ex)


def _outproj_ln(x, o_fox, o_conv, o_c, o_s, o_w, small, sm_col, w_out, g, b, alpha, tm):
    m, d = x.shape
    row = lambda w: pl.BlockSpec((tm, w), lambda i: (i, 0))
    full = lambda shp: pl.BlockSpec(shp, lambda i: (0,) * len(shp))
    nw = H_NSA * HEAD_DIM
    return pl.pallas_call(
        functools.partial(_outproj_kernel, alpha=alpha),
        grid=(m // tm,),
        in_specs=[row(d), row(H_FOX * HEAD_DIM), row(D_CONV), row(nw), row(nw), row(nw),
                  pl.BlockSpec((tm, LANES), lambda i: (i, sm_col // LANES)),
                  full((3, LANES, nw)), full(w_out.shape), full((1, d)), full((1, d))],
        out_specs=row(d),
        out_shape=jax.ShapeDtypeStruct((m, d), f32),
        compiler_params=_cparams(("parallel",)),
        name="outproj_ln",
    )(x, o_fox, o_conv, o_c, o_s, o_w, small, _gate_expand(), w_out, g.reshape(1, d), b.reshape(1, d))


def _swiglu_partial(xb, w1, w3, w2):
    h1 = jnp.dot(xb, w1, preferred_element_type=f32)
    h3 = jnp.dot(xb, w3, preferred_element_type=f32)
    hh = (h1 * _sigmoid(h1) * h3).astype(bf16)
    return jnp.dot(hh, w2, preferred_element_type=f32)


def _ffn_kernel(x_ref, w1_ref, w3_ref, w2_ref, g_ref, b_ref, o_ref, acc_ref, *, alpha):
    f = pl.program_id(1)

    @pl.when(f == 0)
    def _():
        acc_ref[...] = jnp.zeros_like(acc_ref)

    acc_ref[...] += _swiglu_partial(x_ref[...].astype(bf16), w1_ref[...], w3_ref[...], w2_ref[...])

    @pl.when(f == pl.num_programs(1) - 1)
    def _():
        o_ref[...] = _layernorm(alpha * x_ref[...] + acc_ref[...], g_ref[...], b_ref[...])


def _ffn_ln(x, w1, w3, w2, g, b, alpha, tm, tf):
    m, d = x.shape
    ff = w1.shape[1]
    return pl.pallas_call(
        functools.partial(_ffn_kernel, alpha=alpha),
        grid=(m // tm, ff // tf),
        in_specs=[pl.BlockSpec((tm, d), lambda i, f: (i, 0)),
                  pl.BlockSpec((d, tf), lambda i, f: (0, f)),
                  pl.BlockSpec((d, tf), lambda i, f: (0, f)),
                  pl.BlockSpec((tf, d), lambda i, f: (f, 0)),
                  pl.BlockSpec((1, d), lambda i, f: (0, 0)),
                  pl.BlockSpec((1, d), lambda i, f: (0, 0))],
        out_specs=pl.BlockSpec((tm, d), lambda i, f: (i, 0)),
        out_shape=jax.ShapeDtypeStruct((m, d), f32),
        scratch_shapes=[pltpu.VMEM((tm, d), f32)],
        compiler_params=_cparams(("parallel", "arbitrary")),
        name="ffn_ln",
    )(x, w1, w3, w2, g.reshape(1, d), b.reshape(1, d))


def _moe_kernel(x_ref, r_ref, w1_ref, w3_ref, w2_ref, g_ref, b_ref, o_ref, acc_ref, gate_ref, *, alpha):
    e, f = pl.program_id(1), pl.program_id(2)
    lane = lax.broadcasted_iota(jnp.int32, (1, LANES), 1)

    @pl.when(jnp.logical_and(e == 0, f == 0))
    def _():
        acc_ref[...] = jnp.zeros_like(acc_ref)
        logits = jnp.dot(x_ref[...], r_ref[...], precision=HIGHEST, preferred_element_type=f32)
        logits = jnp.where(lane < N_EXPERTS, logits, -jnp.inf)
        mx = jnp.max(logits, axis=1, keepdims=True)
        ex = jnp.exp(logits - mx)
        probs = ex / jnp.sum(ex, axis=1, keepdims=True)
        lane_f = lane.astype(f32)
        work = jnp.where(lane < N_EXPERTS, probs, -1.0)
        picked = jnp.zeros_like(probs)
        for _ in range(TOP_K):
            top = jnp.max(work, axis=1, keepdims=True)
            ix = jnp.min(jnp.where(work == top, lane_f, 1e9), axis=1, keepdims=True)
            hit = lane_f == ix
            picked = jnp.where(hit, probs, picked)
            work = jnp.where(hit, -1.0, work)
        gate_ref[...] = picked / jnp.sum(picked, axis=1, keepdims=True)

    part = _swiglu_partial(x_ref[...].astype(bf16), w1_ref[0], w3_ref[0], w2_ref[0])
    gate_e = jnp.sum(jnp.where(lane == e, gate_ref[...], 0.0), axis=1, keepdims=True)
    acc_ref[...] += gate_e * part

    @pl.when(jnp.logical_and(e == pl.num_programs(1) - 1, f == pl.num_programs(2) - 1))
    def _():
        o_ref[...] = _layernorm(alpha * x_ref[...] + acc_ref[...], g_ref[...], b_ref[...])


def _moe_ln(x, router, w1, w3, w2, g, b, alpha, tm, tf):
    m, d = x.shape
    ne, _, ff = w1.shape
    return pl.pallas_call(
        functools.partial(_moe_kernel, alpha=alpha),
        grid=(m // tm, ne, ff // tf),
        in_specs=[pl.BlockSpec((tm, d), lambda i, e, f: (i, 0)),
                  pl.BlockSpec((d, LANES), lambda i, e, f: (0, 0)),
                  pl.BlockSpec((1, d, tf), lambda i, e, f: (e, 0, f)),
                  pl.BlockSpec((1, d, tf), lambda i, e, f: (e, 0, f)),
                  pl.BlockSpec((1, tf, d), lambda i, e, f: (e, f, 0)),
                  pl.BlockSpec((1, d), lambda i, e, f: (0, 0)),
                  pl.BlockSpec((1, d), lambda i, e, f: (0, 0))],
        out_specs=pl.BlockSpec((tm, d), lambda i, e, f: (i, 0)),
        out_shape=jax.ShapeDtypeStruct((m, d), f32),
        scratch_shapes=[pltpu.VMEM((tm, d), f32), pltpu.VMEM((tm, LANES), f32)],
        compiler_params=_cparams(("parallel", "arbitrary", "arbitrary")),
        name="moe_ln",
    )(x, router, w1, w3, w2, g.reshape(1, d), b.reshape(1, d))


def _pack_layer_weights(w_in, b_f, conv_w, w_cmp_k, w_cmp_v):
    d = w_in.shape[0]
    cols = lambda a, b: w_in[:, a:b]
    small = jnp.concatenate([cols(_C_FL, _C_BG), cols(_C_GN, _C_END),
                             jnp.zeros((d, LANES - H_FOX - 3 * H_NSA), w_in.dtype)], axis=1)
    wr = jnp.concatenate([cols(_C_QN, _C_KC), cols(_C_QF, _C_KF), cols(_C_BG, _C_QN), small], axis=1)
    kv_cols = jnp.concatenate([cols(_C_KF, _C_FL), cols(_C_KC, _C_KW)], axis=1)
    win_cols = cols(_C_KW, _C_GN)
    fl = jnp.concatenate([cols(_C_FL, _C_BG), jnp.zeros((d, SUBLANES - H_FOX), w_in.dtype)], axis=1)
    wt = jnp.concatenate([kv_cols, win_cols, fl], axis=1).T
    wr_all = jnp.concatenate([wr, kv_cols, win_cols], axis=1)
    bf8 = jnp.concatenate([b_f, jnp.zeros((SUBLANES - H_FOX,), b_f.dtype)]).reshape(SUBLANES, 1)
    w8 = jnp.concatenate([conv_w, jnp.zeros((SUBLANES - CONV_K, D_CONV), conv_w.dtype)], axis=0)

    def wc_of(w):
        lo = w[:CMP_STRIDE * HEAD_DIM].reshape(CMP_STRIDE, HEAD_DIM, HEAD_DIM)
        hi = w[CMP_STRIDE * HEAD_DIM:].reshape(CMP_STRIDE, HEAD_DIM, HEAD_DIM)
        return jnp.concatenate([lo, hi], axis=-1)

    wck, wcv = wc_of(w_cmp_k), wc_of(w_cmp_v)
    wc = jnp.stack([wck] * N_KV_NSA + [wcv] * N_KV_NSA).astype(bf16)
    return dict(wr=wr.astype(bf16), wr_all=wr_all.astype(bf16), wt=wt.astype(bf16), bf8=bf8, w8=w8, wc=wc)


def _tile(t, want):
    return min(t, want)


def _prompt_mixer(x, pk, w_out_b, tiles):
    t = x.shape[0]
    pr, pt = _project(x, pk["wr"], pk["wt"], _tile(t, tiles["proj"]))
    logf_t, c_t = _logf_cumsum(pt, pk["bf8"], _tile(t, 2048))
    tq = _tile(t, tiles["flash"])
    o_fox = _flash("fox", pr, pt, R_QF, T_KF, T_VF, tq, (c_t,))
    hh = _compress(pt, pk["wc"], _tile(t, 2048))
    tq_cmp = _tile(t, tiles["cmp"])
    o_c, sel, _, anyb = _cmp_select(pr, R_QN, hh, 0, t, tq_cmp)
    o_s = _flash("slc", pr, pt, R_QN, T_KS, T_VS, tq, (sel, _slc_tile_flags(anyb, tq, tq_cmp, t)))
    o_w = _flash("win", pr, pt, R_QN, T_KW, T_VW, _tile(t, WINDOW))
    state8 = jnp.zeros((SUBLANES, D_CONV), f32)
    o_conv, u_last = _conv_prompt(pr, state8, pk["w8"], _tile(t, tiles["proj"]))
    return pr, pt, logf_t, o_fox, o_conv, o_c, o_s, o_w, u_last


def _trunk_prompt(x, layers, alpha, tiles):
    t = x.shape[0]
    kv, logf, conv, win = [], [], [], []
    for lw in layers:
        pr, pt, logf_t, o_fox, o_conv, o_c, o_s, o_w, u_last = _prompt_mixer(x, lw["pk"], lw["w_out"], tiles)
        tm = _tile(t, tiles["row"])
        x = _outproj_ln(x, o_fox, o_conv, o_c, o_s, o_w, pr, R_SM, lw["w_out"], lw["ln1_g"], lw["ln1_b"], alpha, tm)
        if lw["moe"] is None:
            w1, w3, w2 = lw["dense"]
            x = _ffn_ln(x, w1, w3, w2, lw["ln2_g"], lw["ln2_b"], alpha, _tile(t, tiles["ffn"]), tiles["tf"])
        else:
            rw, w1, w3, w2 = lw["moe"]
            x = _moe_ln(x, rw, w1, w3, w2, lw["ln2_g"], lw["ln2_b"], alpha, _tile(t, tiles["ffn"]), tiles["tf"])
        kv.append(pt[:N_KV_FEAT].reshape(N_KV_FEAT // HEAD_DIM, HEAD_DIM, t))
        logf.append(logf_t[:H_FOX])
        conv.append(u_last[SUBLANES - (CONV_K - 1):])
        keep = min(WINDOW, t)
        win.append(pt[T_KW:T_FL, t - keep:].reshape(2 * N_KV_NSA, HEAD_DIM, keep))
    kv = jnp.transpose(jnp.stack(kv), (0, 3, 1, 2))[:, None]
    logf = jnp.transpose(jnp.stack(logf), (0, 2, 1))[:, None]
    conv = jnp.stack(conv)[:, None]
    win = jnp.transpose(jnp.stack(win), (0, 3, 1, 2))[:, None]
    return x, kv, logf, conv, win


TILES = dict(proj=512, flash=512, cmp=256, row=512, ffn=1024, tf=1408, pg=16)


def _prepare_layers(w_in, b_f, conv_w, w_cmp_k, w_cmp_v, w_out, ln1_g, ln1_b, ln2_g, ln2_b,
                    dense_w1, dense_w3, dense_w2, router_w, moe_w1, moe_w3, moe_w2):
    layers = []
    d = w_in.shape[1]
    for l in range(w_in.shape[0]):
        lw = dict(pk=_pack_layer_weights(w_in[l], b_f[l], conv_w[l], w_cmp_k[l], w_cmp_v[l]),
                  w_out=w_out[l].astype(bf16), ln1_g=ln1_g[l], ln1_b=ln1_b[l], ln2_g=ln2_g[l], ln2_b=ln2_b[l],
                  dense=None, moe=None)
        if l % 2 == 0:
            lw["dense"] = (dense_w1[l // 2].astype(bf16), dense_w3[l // 2].astype(bf16), dense_w2[l // 2].astype(bf16))
        else:
            rw = jnp.concatenate([router_w[l // 2], jnp.zeros((d, LANES - N_EXPERTS), f32)], axis=1)
            lw["moe"] = (rw, moe_w1[l // 2].astype(bf16), moe_w3[l // 2].astype(bf16), moe_w2[l // 2].astype(bf16))
        layers.append(lw)
    return layers


R_KV, R_WIN, R_ALL_END = R_END, R_END + N_KV_FEAT, R_END + N_KV_FEAT + 2 * N_KV_NSA * HEAD_DIM
PAGE = 128
ROW_FOX, ROW_CMP, ROW_SLC = 0, 2 * H_FOX, 2 * H_FOX + 2 * N_KV_NSA
FOX_W = H_FOX * HEAD_DIM
Q_ROWS = 16
SEQS_PER_STEP = 2


def _online_update(s, v_t, m_ref, l_ref, acc_ref):
    m_old = m_ref[...]
    m_new = jnp.maximum(m_old, jnp.max(s, axis=1, keepdims=True))
    alpha = jnp.exp(m_old - m_new)
    p = jnp.exp(s - m_new)
    l_ref[...] = alpha * l_ref[...] + jnp.sum(p, axis=1, keepdims=True)
    acc_ref[...] = alpha * acc_ref[...] + _nt_dot(p.astype(bf16), v_t)
    m_ref[...] = m_new


def _decode_kernel(pt_ref, qbd_ref, knew_ref, vnew_ref, cq_ref, ck_ref, wc_ref, *rest, pg, sp):
    page_refs, lf_refs = rest[:sp * pg], rest[sp * pg:2 * sp * pg]
    perm_ref, o_ref, hh_ref, m_ref, l_ref, acc_ref, carry_ref, at_ref, kt_s, vt_s, lf_s = rest[2 * sp * pg:]
    j = pl.program_id(1)
    width = pg * PAGE

    @pl.when(j == 0)
    def _():
        m_ref[...] = jnp.full_like(m_ref, NEG)
        l_ref[...] = jnp.zeros_like(l_ref)
        acc_ref[...] = jnp.zeros_like(acc_ref)
        carry_ref[...] = jnp.zeros_like(carry_ref)
        lf_s[...] = jnp.zeros_like(lf_s)

    row_head = lax.rem(lax.broadcasted_iota(jnp.int32, (Q_ROWS, 1), 0), H_FOX)
    lane = lax.broadcasted_iota(jnp.int32, (1, width), 1)
    n_half = pg * (PAGE // CMP_STRIDE)
    halves_per_page = PAGE // CMP_STRIDE
    perm = perm_ref[...]
    for sq in range(sp):
        pages = page_refs[sq * pg:(sq + 1) * pg]
        qbd = (qbd_ref[sq] * 0.125).astype(bf16)
        cq = cq_ref[sq]
        for k in range(pg):
            sl = slice(k * PAGE, (k + 1) * PAGE)
            kt_s[sq, :, sl] = pages[k][0, 0, 0:H_FOX].reshape(FOX_W, PAGE).astype(bf16)
            vt_s[sq, :, sl] = pages[k][0, 0, H_FOX:2 * H_FOX].reshape(FOX_W, PAGE).astype(bf16)
            lf_s[sq, 0:H_FOX, sl] = lf_refs[sq * pg + k][0, 0]
        lf = lf_s[sq]
        incl = lf
        shift = 1
        while shift < width:
            incl = incl + jnp.where(lane + shift < width, pltpu.roll(incl, width - shift, 1), 0.0)
            shift *= 2
        suffix = carry_ref[sq, :, 0:1] + (incl - lf)
        carry_ref[sq] = carry_ref[sq] + incl[:, 0:1]
        bias = jnp.zeros((Q_ROWS, width), f32)
        for h in range(H_FOX):
            bias = jnp.where(row_head == h, suffix[h:h + 1, :], bias)
        s = jnp.dot(qbd, kt_s[sq], preferred_element_type=f32) + bias + cq
        _online_update(s, vt_s[sq], m_ref.at[sq], l_ref.at[sq], acc_ref.at[sq])
        for kind in range(2 * N_KV_NSA):
            slot = sq * 2 * N_KV_NSA + kind
            for k in range(pg):
                tile = pages[k][0, 0, ROW_CMP + kind].astype(bf16)
                at_ref[slot, k * PAGE:(k + 1) * PAGE, :] = _nt_dot(perm, tile)
            acc = jnp.zeros((n_half, 2 * HEAD_DIM), f32)
            for r in range(CMP_STRIDE):
                rows = jnp.concatenate(
                    [at_ref[slot, k * PAGE + r * halves_per_page:k * PAGE + (r + 1) * halves_per_page, :]
                     for k in range(pg)], axis=0)
                acc = acc + jnp.dot(rows.astype(bf16), wc_ref[kind, r], preferred_element_type=f32)
            hh_ref[sq, kind] = acc

    @pl.when(j == pl.num_programs(1) - 1)
    def _():
        trow = lax.shift_right_logical(lax.broadcasted_iota(jnp.int32, (Q_ROWS, 1), 0), 2)
        tcol = lax.broadcasted_iota(jnp.int32, (1, SUBLANES), 1)
        for sq in range(sp):
            qbd = (qbd_ref[sq] * 0.125).astype(bf16)
            s = _nt_dot(qbd, knew_ref[sq].astype(bf16))
            s = jnp.where(tcol <= trow, s + (cq_ref[sq] - ck_ref[sq]), NEG)
            _online_update(s, vnew_ref[sq].T.astype(bf16), m_ref.at[sq], l_ref.at[sq], acc_ref.at[sq])
            o_ref[sq] = acc_ref[sq] / l_ref[sq]


def _decode(l, page_table, qbd, knew, vnew, cq, ck, wc, cache_t, logf_t, pg, sp):
    n, n_pages = page_table.shape
    steps = n_pages // pg

    def page_of(b, j, sq, k, pt):
        return pt[b * sp + sq, (steps - 1 - j) * pg + k]

    per_seq = lambda shp: pl.BlockSpec((sp,) + shp, lambda b, j, pt: (b,) + (0,) * len(shp))
    const = lambda shp: pl.BlockSpec(shp, lambda b, j, pt: (0,) * len(shp))
    page_specs = [pl.BlockSpec((1, 1, ROW_SLC, HEAD_DIM, PAGE),
                               lambda b, j, pt, sq=sq, k=k: (l, page_of(b, j, sq, k, pt), 0, 0, 0))
                  for sq in range(sp) for k in range(pg)]
    lf_specs = [pl.BlockSpec((1, 1, H_FOX, PAGE), lambda b, j, pt, sq=sq, k=k: (l, page_of(b, j, sq, k, pt), 0, 0))
                for sq in range(sp) for k in range(pg)]
    ii, jj = np.meshgrid(np.arange(PAGE), np.arange(PAGE), indexing="ij")
    halves = PAGE // CMP_STRIDE
    perm = jnp.asarray((jj == CMP_STRIDE * (ii % halves) + ii // halves).astype(np.float32)).astype(bf16)
    n_half = n_pages * (PAGE // CMP_STRIDE)
    grid_spec = pltpu.PrefetchScalarGridSpec(
        num_scalar_prefetch=1,
        grid=(n // sp, steps),
        in_specs=[per_seq((Q_ROWS, FOX_W)), per_seq((SUBLANES, FOX_W)), per_seq((SUBLANES, FOX_W)),
                  per_seq((Q_ROWS, 1)), per_seq((Q_ROWS, SUBLANES)),
                  const(wc.shape)] + page_specs + lf_specs + [const((PAGE, PAGE))],
        out_specs=[per_seq((Q_ROWS, FOX_W)),
                   pl.BlockSpec((sp, 2 * N_KV_NSA, pg * (PAGE // CMP_STRIDE), 2 * HEAD_DIM),
                                lambda b, j, pt: (b, 0, steps - 1 - j, 0))],
        scratch_shapes=[pltpu.VMEM((sp, Q_ROWS, 1), f32), pltpu.VMEM((sp, Q_ROWS, 1), f32),
                        pltpu.VMEM((sp, Q_ROWS, FOX_W), f32), pltpu.VMEM((sp, SUBLANES, PAGE), f32),
                        pltpu.VMEM((sp * 2 * N_KV_NSA, pg * PAGE, HEAD_DIM), f32),
                        pltpu.VMEM((sp, FOX_W, pg * PAGE), bf16), pltpu.VMEM((sp, FOX_W, pg * PAGE), bf16),
                        pltpu.VMEM((sp, SUBLANES, pg * PAGE), f32)])
    return pl.pallas_call(
        functools.partial(_decode_kernel, pg=pg, sp=sp),
        grid_spec=grid_spec,
        out_shape=[jax.ShapeDtypeStruct((n, Q_ROWS, FOX_W), f32),
                   jax.ShapeDtypeStruct((n, 2 * N_KV_NSA, n_half, 2 * HEAD_DIM), f32)],
        compiler_params=_cparams(("parallel", "arbitrary")),
        name="decode_fox_compress",
    )(page_table, qbd, knew, vnew, cq, ck, wc, *([cache_t] * (sp * pg)), *([logf_t] * (sp * pg)), perm)


def _softmax_two(s_list, v_list, sn, vn):
    m = jnp.max(sn, axis=1, keepdims=True)
    for s in s_list:
        m = jnp.maximum(m, jnp.max(s, axis=1, keepdims=True))
    pn = jnp.exp(sn - m)
    den = jnp.sum(pn, axis=1, keepdims=True)
    out = jnp.zeros((sn.shape[0], vn.shape[1]), f32)
    for c in range(vn.shape[0]):
        out = out + pn[:, c:c + 1] * vn[c:c + 1, :]
    for s, v in zip(s_list, v_list):
        p = jnp.exp(s - m)
        den = den + jnp.sum(p, axis=1, keepdims=True)
        out = out + _nt_dot(p.astype(bf16), v)
    return out / den


def _new_token_scores(qg, kn, slope_col, t):
    tcol = lax.broadcasted_iota(jnp.int32, (1, SUBLANES), 1)
    sn = jnp.zeros((SUBLANES, SUBLANES), f32)
    for c in range(SUBLANES):
        sn = jnp.where(tcol == c, jnp.sum(qg * kn[c:c + 1, :], axis=1, keepdims=True), sn)
    return jnp.where(tcol <= t, sn - slope_col * (t - tcol).astype(f32), NEG)


def _nsa_decode_kernel(idx_ref, pt_ref, qs_ref, ksn_ref, vsn_ref, kwn_ref, vwn_ref, win_ref, *rest, past):
    n_k = N_KV_NSA * TOP_N
    k_refs, v_refs = rest[:n_k], rest[n_k:2 * n_k]
    os_ref, ow_ref = rest[2 * n_k:]
    b, t = pl.program_id(0), pl.program_id(1)
    qpos = past + t
    lane = lax.broadcasted_iota(jnp.int32, (1, PAGE), 1)
    lane_half = lax.shift_right_logical(lane, 6)
    rows = lax.broadcasted_iota(jnp.int32, (SUBLANES, 1), 0)
    n_past_blocks = past // SLC_BLOCK
    wcol = lax.broadcasted_iota(jnp.int32, (1, WINDOW), 1)
    for g in range(N_KV_NSA):
        slope_col = jnp.zeros((SUBLANES, 1), f32)
        for r in range(HEADS_PER_GROUP):
            slope_col = jnp.where(rows == r, _alibi_slope(g * HEADS_PER_GROUP + r), slope_col)
        qg = qs_ref[0, 0, g] * 0.125
        qgb = qg.astype(bf16)
        s_list, v_list = [], []
        for k in range(TOP_N):
            blk = idx_ref[b, t, g * TOP_N + k]
            half = lax.rem(blk, 2)
            kpos = (blk // 2) * PAGE + lane
            ok = jnp.logical_and(lane_half == half, blk < n_past_blocks)
            s = jnp.dot(qgb, k_refs[g * TOP_N + k][0, 0, 0].astype(bf16), preferred_element_type=f32)
            s_list.append(jnp.where(ok, s - slope_col * (qpos - kpos).astype(f32), NEG))
            v_list.append(v_refs[g * TOP_N + k][0, 0, 0].astype(bf16))
        sn = _new_token_scores(qg, ksn_ref[0, g], slope_col, t)
        os_ref[0, 0, g] = _softmax_two(s_list, v_list, sn, vsn_ref[0, g])
        d = WINDOW + t - wcol
        sw = jnp.dot(qgb, win_ref[0, 0, g].astype(bf16), preferred_element_type=f32)
        sw = jnp.where(d <= WINDOW, sw - slope_col * d.astype(f32), NEG)
        swn = _new_token_scores(qg, kwn_ref[0, g], slope_col, t)
        ow_ref[0, 0, g] = _softmax_two([sw], [win_ref[0, 0, N_KV_NSA + g].astype(bf16)], swn, vwn_ref[0, g])


def _nsa_decode(l, idx, page_table, qs, ksn, vsn, kwn, vwn, win_t, cache_t, past):
    n, t_new = idx.shape[0], idx.shape[1]
    n_pages = page_table.shape[1]

    def sel_map(g, k, row0):
        def f(b, t, idx_r, pt_r):
            page = jnp.minimum(idx_r[b, t, g * TOP_N + k] // 2, n_pages - 1)
            return (l, pt_r[b, page], row0 + g, 0, 0)
        return f

    tile = (1, 1, 1, HEAD_DIM, PAGE)
    k_specs = [pl.BlockSpec(tile, sel_map(g, k, ROW_SLC)) for g in range(N_KV_NSA) for k in range(TOP_N)]
    v_specs = [pl.BlockSpec(tile, sel_map(g, k, ROW_SLC + N_KV_NSA)) for g in range(N_KV_NSA) for k in range(TOP_N)]
    new_spec = pl.BlockSpec((1, N_KV_NSA, SUBLANES, HEAD_DIM), lambda b, t, i, p: (b, 0, 0, 0))
    q_spec = pl.BlockSpec((1, 1, N_KV_NSA, SUBLANES, HEAD_DIM), lambda b, t, i, p: (b, t, 0, 0, 0))
    grid_spec = pltpu.PrefetchScalarGridSpec(
        num_scalar_prefetch=2,
        grid=(n, t_new),
        in_specs=[q_spec, new_spec, new_spec, new_spec, new_spec,
                  pl.BlockSpec((1, 1, 2 * N_KV_NSA, HEAD_DIM, WINDOW), lambda b, t, i, p: (l, b, 0, 0, 0))] + k_specs + v_specs,
        out_specs=[q_spec, q_spec])
    n_k = N_KV_NSA * TOP_N
    shp = jax.ShapeDtypeStruct((n, t_new, N_KV_NSA, SUBLANES, HEAD_DIM), f32)
    return pl.pallas_call(
        functools.partial(_nsa_decode_kernel, past=past),
        grid_spec=grid_spec,
        out_shape=[shp, shp],
        compiler_params=_cparams(("parallel", "arbitrary")),
        name="decode_slc_win",
    )(idx, page_table, qs, ksn, vsn, kwn, vwn, win_t, *([cache_t] * (2 * n_k)))


def _conv_sample_kernel(b_ref, c_ref, u_ref, w_ref, o_ref, un_ref):
    u = c_ref[...] * u_ref[...]
    w = w_ref[...]
    y = w[0:1] * pltpu.roll(u, 2, 0) + w[1:2] * pltpu.roll(u, 1, 0) + w[2:3] * u
    o_ref[...] = b_ref[...] * y
    un_ref[...] = u


def _conv_sample(b_ext, c_ext, u_ext, w8):
    shp = jax.ShapeDtypeStruct(b_ext.shape, f32)
    return pl.pallas_call(_conv_sample_kernel, out_shape=[shp, shp], name="conv_sample")(b_ext, c_ext, u_ext, w8)


def _sample_mixer(x, lw, l, cache_t, logf_t, win_t, conv_state, page_table, past, n, t_new, pg):
    pk = lw["pk"]
    m = n * t_new
    assert t_new * H_FOX == Q_ROWS and m == LANES and past % PAGE == 0 and past >= WINDOW
    pr, pt = _project(x, pk["wr_all"], pk["wt"], m)
    lf_t, cum_t = _logf_cumsum(pt, pk["bf8"], m, segment=t_new)
    seq = lambda a: a.reshape(n, t_new, -1)
    pad8 = lambda a: jnp.pad(a, ((0, 0), (0, SUBLANES - t_new), (0, 0)))
    kv_rows = pr[:, R_KV:R_WIN]
    win_rows = pr[:, R_WIN:R_ALL_END]
    head_mask = jnp.asarray((np.arange(FOX_W)[None, :] // HEAD_DIM == np.arange(H_FOX)[:, None]).astype(np.float32))
    qbd = (seq(pr[:, R_QF:R_QF + FOX_W])[:, :, None, :] * head_mask[None, None]).reshape(n, Q_ROWS, FOX_W)
    knew, vnew = pad8(seq(kv_rows[:, 0:FOX_W])), pad8(seq(kv_rows[:, FOX_W:2 * FOX_W]))
    cum = cum_t[:H_FOX].reshape(H_FOX, n, t_new)
    cq = jnp.transpose(cum, (1, 2, 0)).reshape(n, Q_ROWS, 1)
    ck = jnp.broadcast_to(jnp.transpose(cum, (1, 0, 2))[:, None], (n, t_new, H_FOX, t_new)).reshape(n, Q_ROWS, t_new)
    ck = jnp.pad(ck, ((0, 0), (0, 0), (0, SUBLANES - t_new)))
    o16, hh = _decode(l, page_table, qbd, knew, vnew, cq, ck, pk["wc"], cache_t, logf_t, pg, SEQS_PER_STEP)
    o16 = o16.reshape(n, t_new, H_FOX, H_FOX, HEAD_DIM)
    o_fox = jnp.stack([o16[:, :, h, h] for h in range(H_FOX)], axis=2).reshape(m, FOX_W)
    qn = pr[:, R_QN:R_QN + H_NSA * HEAD_DIM]
    q8 = pad8(seq(qn)).reshape(n * SUBLANES, H_NSA * HEAD_DIM)
    o_c8, _, idx8, _ = _cmp_select(q8, 0, hh, past, t_new, SUBLANES)
    o_c = o_c8.reshape(n, SUBLANES, -1)[:, :t_new].reshape(m, -1)
    idx = idx8.reshape(n, SUBLANES, LANES)[:, :t_new, :N_KV_NSA * TOP_N]
    qs = qn.reshape(n, t_new, N_KV_NSA, HEADS_PER_GROUP, HEAD_DIM)
    qs = jnp.pad(qs, ((0, 0), (0, 0), (0, 0), (0, SUBLANES - HEADS_PER_GROUP), (0, 0)))
    grp = lambda a: jnp.pad(jnp.transpose(a.reshape(n, t_new, N_KV_NSA, HEAD_DIM), (0, 2, 1, 3)),
                            ((0, 0), (0, 0), (0, SUBLANES - t_new), (0, 0)))
    gw = N_KV_NSA * HEAD_DIM
    ksn, vsn = grp(kv_rows[:, 6 * gw:7 * gw]), grp(kv_rows[:, 7 * gw:8 * gw])
    kwn, vwn = grp(win_rows[:, 0:gw]), grp(win_rows[:, gw:2 * gw])
    o_s5, o_w5 = _nsa_decode(l, idx, page_table, qs, ksn, vsn, kwn, vwn, win_t, cache_t, past)
    heads = lambda a: a[:, :, :, :HEADS_PER_GROUP].reshape(m, H_NSA * HEAD_DIM)
    o_s, o_w = heads(o_s5), heads(o_w5)
    ext = lambda a, head: jnp.concatenate([head, seq(a)], axis=1).reshape(n * SUBLANES, D_CONV)
    zeros2 = jnp.zeros((n, 2, D_CONV), f32)
    one_head = jnp.concatenate([zeros2, jnp.ones((n, 2, D_CONV), f32)], axis=1)
    b_ext = ext(pr[:, R_BG:R_BG + D_CONV], jnp.zeros((n, 4, D_CONV), f32))
    c_ext = ext(pr[:, R_CG:R_CG + D_CONV], one_head)
    u_ext = ext(pr[:, R_UIN:R_UIN + D_CONV], jnp.concatenate([zeros2, conv_state], axis=1))
    o_conv8, u8 = _conv_sample(b_ext, c_ext, u_ext, pk["w8"])
    o_conv = o_conv8.reshape(n, SUBLANES, D_CONV)[:, SUBLANES - t_new:].reshape(m, D_CONV)
    new_conv = u8.reshape(n, SUBLANES, D_CONV)[:, SUBLANES - (CONV_K - 1):]
    new_kv = kv_rows.reshape(n, t_new, N_KV_FEAT // HEAD_DIM, HEAD_DIM)
    new_logf = jnp.transpose(lf_t[:H_FOX].reshape(H_FOX, n, t_new), (1, 2, 0))
    win_new_t = jnp.transpose(pt[T_KW:T_FL].reshape(2 * N_KV_NSA, HEAD_DIM, n, t_new), (2, 0, 1, 3))
    new_win_t = jnp.concatenate([win_t[l][..., t_new:], win_new_t], axis=-1)
    return pr, o_fox, o_conv, o_c, o_s, o_w, new_kv, new_logf, new_conv, new_win_t


def _trunk_sample(x, layers, alpha, cache_t, logf_t, win_t, state_conv, page_table, tiles):
    n, t_new, d = x.shape
    past = page_table.shape[1] * PAGE
    x = x.reshape(n * t_new, d)
    m = n * t_new
    pg = min(tiles["pg"], page_table.shape[1])
    kv, logf, conv, win = [], [], [], []
    for l, lw in enumerate(layers):
        pr, o_fox, o_conv, o_c, o_s, o_w, nkv, nlogf, nconv, nwin_t = _sample_mixer(
            x, lw, l, cache_t, logf_t, win_t, state_conv[l], page_table, past, n, t_new, pg)
        x = _outproj_ln(x, o_fox, o_conv, o_c, o_s, o_w, pr, R_SM, lw["w_out"], lw["ln1_g"], lw["ln1_b"], alpha, m)
        if lw["moe"] is None:
            w1, w3, w2 = lw["dense"]
            x = _ffn_ln(x, w1, w3, w2, lw["ln2_g"], lw["ln2_b"], alpha, m, tiles["tf"])
        else:
            rw, w1, w3, w2 = lw["moe"]
            x = _moe_ln(x, rw, w1, w3, w2, lw["ln2_g"], lw["ln2_b"], alpha, m, tiles["tf"])
        kv.append(nkv)
        logf.append(nlogf)
        conv.append(nconv)
        win.append(jnp.transpose(nwin_t, (0, 3, 1, 2)))
    return x.reshape(n, t_new, d), jnp.stack(kv), jnp.stack(logf), jnp.stack(conv), jnp.stack(win)


def kernel(x_prompt, x_sample, cache_kv, cache_logf, state_conv, state_win, page_table, w_in, b_f, conv_w, w_cmp_k, w_cmp_v, w_out, ln1_g, ln1_b, ln2_g, ln2_b, dense_w1, dense_w3, dense_w2, router_w, moe_w1, moe_w3, moe_w2):
    depth = w_in.shape[0]
    alpha = (2.0 * depth) ** 0.25
    layers = _prepare_layers(w_in, b_f, conv_w, w_cmp_k, w_cmp_v, w_out, ln1_g, ln1_b, ln2_g, ln2_b,
                             dense_w1, dense_w3, dense_w2, router_w, moe_w1, moe_w3, moe_w2)
    y_p, kv_p, logf_p, conv_p, win_p = _trunk_prompt(x_prompt[0], layers, alpha, TILES)
    cache_t = jnp.transpose(cache_kv, (0, 1, 3, 4, 2))
    logf_t = jnp.transpose(cache_logf, (0, 1, 3, 2))
    win_t = jnp.transpose(state_win, (0, 1, 3, 4, 2))
    y_s, kv_s, logf_s, conv_s, win_s = _trunk_sample(
        x_sample, layers, alpha, cache_t, logf_t, win_t, state_conv, page_table, TILES)
    return (y_p[None], y_s, kv_p, logf_p, conv_p, win_p, kv_s, logf_s, conv_s, win_s)
```
